```python
import jax, jax.numpy as jnp
from jax import lax
import numpy as np

D_MODEL = 1024
BATCH = 8
SEQ = 4096
DEPTH = 1

MLA_HEADS = 4
QK_NOPE = 128
QK_ROPE = 64
V_HEAD = 128
Q_LORA = 256
KV_LORA = 256
QK_HEAD = QK_NOPE + QK_ROPE
MLA_WIDTH = MLA_HEADS * V_HEAD
HG_HEADS = 4
HG_DK = 128
HG_DV = 128
HG_FDIM = HG_HEADS * HG_DK
HG_WIDTH = HG_HEADS * HG_DV
CHUNK = 64
D_MIX = MLA_WIDTH + HG_WIDTH
D_FF = -(-8 * D_MODEL // (3 * 256)) * 256
PLE_DIM = 256
ROPE_THETA = 10000.0
EPS = 1e-6
Q_BLOCK = 128
IN_SIZES = (Q_LORA, KV_LORA, QK_ROPE, HG_FDIM, HG_FDIM, HG_FDIM, HG_WIDTH, HG_WIDTH)
D_IN = sum(IN_SIZES)
IN_SPLITS = tuple(int(v) for v in np.cumsum(IN_SIZES)[:-1])

kernel_name = "hymba_mla_hgrn2_ple_block"


def rms_norm(x, g):
    xf = x.astype(jnp.float32)
    y = xf * lax.rsqrt(jnp.mean(xf * xf, axis=-1, keepdims=True) + EPS)
    return (y * g.astype(jnp.float32)).astype(x.dtype)


def rope(x, cos, sin):
    x1, x2 = jnp.split(x.astype(jnp.float32), 2, axis=-1)
    return jnp.concatenate([x1 * cos - x2 * sin, x1 * sin + x2 * cos], axis=-1).astype(x.dtype)


def blocked_attention(q, k, v):
    B, S, H, D = q.shape
    nb = S // Q_BLOCK
    qb = q.reshape(B, nb, Q_BLOCK, H, D).transpose(1, 0, 2, 3, 4)
    scale = QK_HEAD ** -0.5

    def one_block(q_blk):
        s = jnp.einsum('bqhd,bkhd->bhqk', q_blk, k, preferred_element_type=jnp.float32) * scale
        w = jax.nn.softmax(s, axis=-1).astype(v.dtype)
        return jnp.einsum('bhqk,bkhd->bqhd', w, v)

    o = lax.map(one_block, qb)
    return o.transpose(1, 0, 2, 3, 4).reshape(B, S, H * v.shape[-1])


def mla(c_q, c_kv, k_r, cos, sin, g_qa, g_kva, w_qb, w_kvb, g_qn, g_kn):
    B, S, _ = c_q.shape
    q = (rms_norm(c_q, g_qa) @ w_qb).reshape(B, S, MLA_HEADS, QK_HEAD)
    kv = (rms_norm(c_kv, g_kva) @ w_kvb).reshape(B, S, MLA_HEADS, QK_NOPE + V_HEAD)
    k_nope, v = kv[..., :QK_NOPE], kv[..., QK_NOPE:]
    k = jnp.concatenate([k_nope, jnp.broadcast_to(k_r[:, :, None, :], (B, S, MLA_HEADS, QK_ROPE))], axis=-1)
    q = rms_norm(q, g_qn)
    k = rms_norm(k, g_kn)
    c, s = cos[:, :, None, :], sin[:, :, None, :]
    q = jnp.concatenate([q[..., :QK_NOPE], rope(q[..., QK_NOPE:], c, s)], axis=-1)
    k = jnp.concatenate([k[..., :QK_NOPE], rope(k[..., QK_NOPE:], c, s)], axis=-1)
    return blocked_attention(q, k, v)


def gla_chunkwise(q, k, v, logf):
    B, H, S, DK = q.shape
    DV = v.shape[-1]
    N = S // CHUNK
    q = q.reshape(B, H, N, CHUNK, DK)
    k = k.reshape(B, H, N, CHUNK, DK)
    v = v.reshape(B, H, N, CHUNK, DV)
    b = jnp.cumsum(logf.reshape(B, H, N, CHUNK, DK), axis=3)
    b_last = b[:, :, :, -1:, :]
    b_mid = b[:, :, :, CHUNK // 2 - 1:CHUNK // 2, :]
    q_intra = q * jnp.exp(b - b_mid)
    k_intra = k * jnp.exp(b_mid - b)
    mask = jnp.tril(jnp.ones((CHUNK, CHUNK), jnp.float32))
    A = jnp.einsum('bhntd,bhnsd->bhnts', q_intra, k_intra) * mask
    o_intra = jnp.einsum('bhnts,bhnsv->bhntv', A, v)
    kv = jnp.einsum('bhnsd,bhnsv->bhndv', k * jnp.exp(b_last - b), v)
    decay = jnp.exp(b_last[:, :, :, 0, :])

    def step(state, inp):
        dec, kv_n = inp
        return dec[..., None] * state + kv_n, state

    _, s_before = lax.scan(step, jnp.zeros((B, H, DK, DV), jnp.float32),
                           (jnp.moveaxis(decay, 2, 0), jnp.moveaxis(kv, 2, 0)))
    s_before = jnp.moveaxis(s_before, 0, 2)
    o_inter = jnp.einsum('bhntd,bhndv->bhntv', q * jnp.exp(b), s_before)
    return (o_intra + o_inter).reshape(B, H, S, DV)


def hgrn2(hq, hf_fwd, hf_bwd, hi, hg, lb, g_out):
    B, S, _ = hq.shape

    def heads(t):
        return t.reshape(B, S, HG_HEADS, -1).transpose(0, 2, 1, 3)

    q = heads(jax.nn.silu(hq.astype(jnp.float32)))
    v = heads(hi.astype(jnp.float32))

    def gates(f_pre, lower):
        f = lower + (1.0 - lower) * jax.nn.sigmoid(f_pre.astype(jnp.float32))
        return heads(1.0 - f), heads(jnp.log(f))

    k_f, lf_f = gates(hf_fwd, lb[0])
    k_b, lf_b = gates(hf_bwd, lb[1])
    o_f = gla_chunkwise(q, k_f, v, lf_f)
    o_b = gla_chunkwise(q[:, :, ::-1], k_b[:, :, ::-1], v[:, :, ::-1], lf_b[:, :, ::-1])[:, :, ::-1]
    o = (o_f + o_b).transpose(0, 2, 1, 3)
    o = rms_norm(o, g_out).reshape(B, S, HG_WIDTH)
    return (o * jax.nn.silu(hg.astype(jnp.float32))).astype(hq.dtype)


def setup_inputs(seed: int = 0) -> dict:
    key = jax.random.key(seed)
    ks = jax.random.split(key, 32)
    f32 = jnp.float32

    def w(k, shape, fan_in):
        return jax.random.normal(k, shape, f32) * (fan_in ** -0.5)

    def gain(k, shape):
        return 1.0 + 0.05 * jax.random.normal(k, shape, f32)

    x = jax.random.normal(ks[0], (BATCH, SEQ, D_MODEL), f32)
    p = jax.random.normal(ks[1], (DEPTH, BATCH, SEQ, PLE_DIM), f32)
    offsets = jax.random.randint(ks[2], (BATCH, 1), 0, 1024, jnp.int32)
    positions = offsets + jnp.arange(SEQ, dtype=jnp.int32)[None, :]
    return {
        "x": x,
        "p": p,
        "positions": positions,
        "g_mix": gain(ks[3], (DEPTH, D_MODEL)),
        "w_in": w(ks[4], (DEPTH, D_MODEL, D_IN), D_MODEL),
        "g_qa": gain(ks[5], (DEPTH, Q_LORA)),
        "g_kva": gain(ks[6], (DEPTH, KV_LORA)),
        "w_qb": w(ks[7], (DEPTH, Q_LORA, MLA_HEADS * QK_HEAD), Q_LORA),
        "w_kvb": w(ks[8], (DEPTH, KV_LORA, MLA_HEADS * (QK_NOPE + V_HEAD)), KV_LORA),
        "g_qn": gain(ks[9], (DEPTH, QK_HEAD)),
        "g_kn": gain(ks[10], (DEPTH, QK_HEAD)),
        "lb_param": 0.1 * jax.random.normal(ks[11], (DEPTH + 1, 2, HG_FDIM), f32),
        "g_hgo": gain(ks[12], (DEPTH, HG_HEADS, HG_DV)),
        "w_o": w(ks[13], (DEPTH, D_MIX, D_MODEL), D_MIX),
        "g_ffn": gain(ks[14], (DEPTH, D_MODEL)),
        "w_gate": w(ks[15], (DEPTH, D_MODEL, D_FF), D_MODEL),
        "w_up": w(ks[16], (DEPTH, D_MODEL, D_FF), D_MODEL),
        "w_down": w(ks[17], (DEPTH, D_FF, D_MODEL), D_FF),
        "g_ple": gain(ks[18], (DEPTH, D_MODEL)),
        "w_ple_gate": w(ks[19], (DEPTH, D_MODEL, D_MODEL), D_MODEL),
        "w_ple_proj": w(ks[20], (DEPTH, PLE_DIM, D_MODEL), PLE_DIM),
    }


def reference(x, p, positions, g_mix, w_in, g_qa, g_kva, w_qb, w_kvb, g_qn, g_kn,
              lb_param, g_hgo, w_o, g_ffn, w_gate, w_up, w_down, g_ple, w_ple_gate, w_ple_proj):
    inv_freq = ROPE_THETA ** (-jnp.arange(0, QK_ROPE, 2, dtype=jnp.float32) / QK_ROPE)
    ang = positions.astype(jnp.float32)[..., None] * inv_freq
    cos, sin = jnp.cos(ang), jnp.sin(ang)
    lower_bounds = jnp.cumsum(jax.nn.softmax(lb_param.astype(jnp.float32), axis=0), axis=0)

    for l in range(DEPTH):
        h = rms_norm(x, g_mix[l])
        z = h @ w_in[l]
        c_q, c_kv, k_r, hq, hf_f, hf_b, hi, hg = jnp.split(z, IN_SPLITS, axis=-1)
        a = mla(c_q, c_kv, k_r, cos, sin, g_qa[l], g_kva[l], w_qb[l], w_kvb[l], g_qn[l], g_kn[l])
        r = hgrn2(hq, hf_f, hf_b, hi, hg, lower_bounds[l], g_hgo[l])
        x = x + jnp.concatenate([a, r], axis=-1) @ w_o[l]
        h = rms_norm(x, g_ffn[l])
        x = x + (jax.nn.silu(h @ w_gate[l]) * (h @ w_up[l])) @ w_down[l]
        gate = jax.nn.sigmoid(rms_norm(x, g_ple[l]) @ w_ple_gate[l])
        x = x + gate * (p[l].astype(x.dtype) @ w_ple_proj[l])
    return x
```

```python
import functools

import jax
import jax.numpy as jnp
import numpy as np
from jax import lax
from jax.experimental import pallas as pl
from jax.experimental.pallas import tpu as pltpu

D_MODEL = 1024
MLA_HEADS = 4
QK_NOPE = 128
QK_ROPE = 64
V_HEAD = 128
Q_LORA = 256
KV_LORA = 256
QK_HEAD = QK_NOPE + QK_ROPE
HG_HEADS = 4
HG_DK = 128
HG_DV = 128
HG_W = HG_HEADS * HG_DK
CHUNK = 64
D_FF = 2816
PLE_DIM = 256
ROPE_THETA = 10000.0
EPS = 1e-6

LANES = 128
QK_PAD = 2 * LANES
VMEM_LIMIT = 56 * 1024 * 1024

BF16 = jnp.bfloat16
F32 = jnp.float32


def _const_spec(shape):
    return pl.BlockSpec(shape, lambda *_: (0,) * len(shape), pipeline_mode=pl.Buffered(1))


def _sigmoid(x):
    return 1.0 / (1.0 + jnp.exp(-x))


def _rms_scale(ss, n):
    return lax.rsqrt(ss * (1.0 / n) + EPS)


def _rope_kernel(pos_ref, freq_ref, cos_ref, sin_ref):
    ang = pos_ref[...].astype(F32) * freq_ref[...]
    cos_ref[...] = jnp.cos(ang)
    sin_ref[...] = jnp.sin(ang)


def _rope_tables(pos_dense, freq_row):
    rows = pos_dense.shape[0]
    blk = 1024
    spec = pl.BlockSpec((blk, LANES), lambda i: (i, 0))
    return pl.pallas_call(
        _rope_kernel,
        grid=(rows // blk,),
        in_specs=[spec, pl.BlockSpec((1, LANES), lambda i: (0, 0))],
        out_specs=[spec, spec],
        out_shape=[jax.ShapeDtypeStruct((rows, LANES), F32)] * 2,
        name="rope_tables",
    )(pos_dense, freq_row)


def _in_proj_kernel(x_ref, cos_ref, sin_ref, gmix_ref, w1_ref, gqa_ref, gkva_ref, wq2_ref, wkv2_ref,
                    gq_ref, gk_ref, lb_ref,
                    q_ref, k_ref, v_ref, hq_ref, hv_ref, kf_ref, kb_ref, lff_ref, lfb_ref, gs_ref):
    x = x_ref[...]
    h = x * _rms_scale(jnp.sum(x * x, axis=-1, keepdims=True), D_MODEL) * gmix_ref[...]
    z = jnp.dot(h.astype(BF16), w1_ref[...], preferred_element_type=F32)

    c_q = z[:, 0:Q_LORA]
    c_kv = z[:, Q_LORA:Q_LORA + KV_LORA]
    o = Q_LORA + KV_LORA
    hq = z[:, o:o + HG_W]
    hff = z[:, o + HG_W:o + 2 * HG_W]
    hfb = z[:, o + 2 * HG_W:o + 3 * HG_W]
    hi = z[:, o + 3 * HG_W:o + 4 * HG_W]
    hg = z[:, o + 4 * HG_W:o + 5 * HG_W]
    o += 5 * HG_W
    k_r = z[:, o:o + LANES]
    k_rs = z[:, o + LANES:o + 2 * LANES]

    cos = cos_ref[...]
    sin = sin_ref[...]

    cqn = c_q * _rms_scale(jnp.sum(c_q * c_q, axis=-1, keepdims=True), Q_LORA) * gqa_ref[...]
    q_all = jnp.dot(cqn.astype(BF16), wq2_ref[...], preferred_element_type=F32)
    gq_n, gq_r, gq_rs = gq_ref[0:1, :], gq_ref[1:2, :], gq_ref[2:3, :]
    scale = QK_HEAD ** -0.5
    nh = MLA_HEADS
    for hd in range(nh):
        qn = q_all[:, hd * LANES:(hd + 1) * LANES]
        qr = q_all[:, (nh + hd) * LANES:(nh + hd + 1) * LANES]
        qrs = q_all[:, (2 * nh + hd) * LANES:(2 * nh + hd + 1) * LANES]
        ss = jnp.sum(qn * qn + qr * qr, axis=-1, keepdims=True)
        r = _rms_scale(ss, QK_HEAD) * scale
        q_ref[:, hd * QK_PAD:hd * QK_PAD + LANES] = (qn * gq_n * r).astype(BF16)
        q_ref[:, hd * QK_PAD + LANES:(hd + 1) * QK_PAD] = (
            (qr * gq_r * cos + qrs * gq_rs * sin) * r).astype(BF16)

    ckvn = c_kv * _rms_scale(jnp.sum(c_kv * c_kv, axis=-1, keepdims=True), KV_LORA) * gkva_ref[...]
    kv_all = jnp.dot(ckvn.astype(BF16), wkv2_ref[...], preferred_element_type=F32)
    gk_n, gk_r, gk_rs = gk_ref[0:1, :], gk_ref[1:2, :], gk_ref[2:3, :]
    k_rot = k_r * gk_r * cos + k_rs * gk_rs * sin
    ss_r = jnp.sum(k_r * k_r, axis=-1, keepdims=True)
    for hd in range(nh):
        kn = kv_all[:, hd * LANES:(hd + 1) * LANES]
        ss = jnp.sum(kn * kn, axis=-1, keepdims=True) + ss_r
        r = _rms_scale(ss, QK_HEAD)
        k_ref[:, hd * QK_PAD:hd * QK_PAD + LANES] = (kn * gk_n * r).astype(BF16)
        k_ref[:, hd * QK_PAD + LANES:(hd + 1) * QK_PAD] = (k_rot * r).astype(BF16)
    v_ref[...] = kv_all[:, nh * LANES:].astype(BF16)

    hq_ref[...] = (hq * _sigmoid(hq)).astype(BF16)
    hv_ref[...] = hi.astype(BF16)
    gs_ref[...] = (hg * _sigmoid(hg)).astype(BF16)
    for f_pre, lb, k_out, lf_out in ((hff, lb_ref[0:1, :], kf_ref, lff_ref),
                                     (hfb, lb_ref[1:2, :], kb_ref, lfb_ref)):
        f = lb + (1.0 - lb) * _sigmoid(f_pre)
        k_out[...] = (1.0 - f).astype(BF16)
        lf_out[...] = jnp.log(f)


def _in_proj(x2, cos_t, sin_t, gmix, w1, gqa, gkva, wq2, wkv2, gq, gk, lb, tm):
    t = x2.shape[0]
    row = lambda w: pl.BlockSpec((tm, w), lambda i: (i, 0))
    outs = [(MLA_HEADS * QK_PAD, BF16), (MLA_HEADS * QK_PAD, BF16), (MLA_HEADS * V_HEAD, BF16),
            (HG_W, BF16), (HG_W, BF16), (HG_W, BF16), (HG_W, BF16), (HG_W, F32), (HG_W, F32), (HG_W, BF16)]
    return pl.pallas_call(
        _in_proj_kernel,
        grid=(t // tm,),
        in_specs=[row(D_MODEL), row(LANES), row(LANES),
                  _const_spec(gmix.shape), _const_spec(w1.shape), _const_spec(gqa.shape),
                  _const_spec(gkva.shape), _const_spec(wq2.shape), _const_spec(wkv2.shape),
                  _const_spec(gq.shape), _const_spec(gk.shape), _const_spec(lb.shape)],
        out_specs=[row(w) for w, _ in outs],
        out_shape=[jax.ShapeDtypeStruct((t, w), dt) for w, dt in outs],
        compiler_params=pltpu.CompilerParams(dimension_semantics=("arbitrary",),
                                             vmem_limit_bytes=VMEM_LIMIT),
        name="in_proj",
    )(x2, cos_t, sin_t, gmix, w1, gqa, gkva, wq2, wkv2, gq, gk, lb)


def _attn_kernel(q_ref, k_ref, v_ref, o_ref):
    s = lax.dot_general(q_ref[...], k_ref[...], (((1,), (1,)), ((), ())),
                        preferred_element_type=F32)
    m = jnp.max(s, axis=-1, keepdims=True)
    p = jnp.exp(s - m)
    l = jnp.sum(p, axis=-1, keepdims=True)
    acc = jnp.dot(p.astype(BF16), v_ref[...], preferred_element_type=F32)
    o_ref[...] = (acc / l).astype(o_ref.dtype)


def _attention(q, k, v, tq):
    b, s, _ = q.shape
    return pl.pallas_call(
        _attn_kernel,
        grid=(b, MLA_HEADS, s // tq),
        in_specs=[pl.BlockSpec((None, tq, QK_PAD), lambda bi, hi, qi: (bi, qi, hi)),
                  pl.BlockSpec((None, s, QK_PAD), lambda bi, hi, qi: (bi, 0, hi)),
                  pl.BlockSpec((None, s, V_HEAD), lambda bi, hi, qi: (bi, 0, hi))],
        out_specs=pl.BlockSpec((None, tq, V_HEAD), lambda bi, hi, qi: (bi, qi, hi)),
        out_shape=jax.ShapeDtypeStruct((b, s, MLA_HEADS * V_HEAD), BF16),
        compiler_params=pltpu.CompilerParams(
            dimension_semantics=("arbitrary", "arbitrary", "arbitrary"),
            vmem_limit_bytes=VMEM_LIMIT),
        name="attention",
    )(q, k, v)


def _hgrn_kernel(qf_ref, vf_ref, kf_ref, lff_ref, qb_ref, vb_ref, kb_ref, lfb_ref, tri_ref,
                 of_ref, ob_ref, st_ref, *, n_chunks):
    @pl.when(pl.program_id(1) == 0)
    def _():
        st_ref[...] = jnp.zeros_like(st_ref)

    t_idx = lax.broadcasted_iota(jnp.int32, (CHUNK, CHUNK), 0)
    s_idx = lax.broadcasted_iota(jnp.int32, (CHUNK, CHUNK), 1)
    streams = (
        (qf_ref, vf_ref, kf_ref, lff_ref, of_ref, t_idx >= s_idx, CHUNK // 2 - 1, CHUNK - 1),
        (qb_ref, vb_ref, kb_ref, lfb_ref, ob_ref, t_idx <= s_idx, CHUNK // 2, 0),
    )

    def body(c, carry):
        for d, (q_ref, v_ref, k_ref, lf_ref, o_ref, mask, mid, last) in enumerate(streams):
            cc = c if d == 0 else n_chunks - 1 - c
            rows = pl.ds(pl.multiple_of(cc * CHUNK, CHUNK), CHUNK)
            lf = lf_ref[rows, :]
            lf_hi = lf.astype(BF16)
            lf_lo = (lf - lf_hi.astype(F32)).astype(BF16)
            tri = tri_ref[d]
            b = (jnp.dot(tri, lf_hi, preferred_element_type=F32)
                 + jnp.dot(tri, lf_lo, preferred_element_type=F32))
            b_mid = b[mid:mid + 1, :]
            b_last = b[last:last + 1, :]
            q = q_ref[rows, :].astype(F32)
            k = k_ref[rows, :].astype(F32)
            q_in = q * jnp.exp(b - b_mid)
            k_in = k * jnp.exp(b_mid - b)
            q_st = (q_in * jnp.exp(b_mid)).astype(BF16)
            k_st = (k_in * jnp.exp(b_last - b_mid)).astype(BF16)
            q_in = q_in.astype(BF16)
            k_in = k_in.astype(BF16)
            decay = jnp.exp(b_last)
            v = v_ref[rows, :]
            for hd in range(HG_HEADS):
                sl = slice(hd * HG_DK, (hd + 1) * HG_DK)
                a = lax.dot_general(q_in[:, sl], k_in[:, sl], (((1,), (1,)), ((), ())),
                                    preferred_element_type=F32)
                a = jnp.where(mask, a, 0.0).astype(BF16)
                st = st_ref[d, hd]
                o = jnp.dot(a, v[:, sl], preferred_element_type=F32)
                o += lax.dot_general(q_st[:, sl], st.astype(BF16), (((1,), (1,)), ((), ())),
                                     preferred_element_type=F32)
                o_ref[rows, sl] = o
                kv_t = lax.dot_general(v[:, sl], k_st[:, sl], (((0,), (0,)), ((), ())),
                                       preferred_element_type=F32)
                st_ref[d, hd] = st * decay[:, sl] + kv_t
        return carry

    lax.fori_loop(0, n_chunks, body, 0)


def _hgrn(hq, hv, kf, kb, lff, lfb, tri, blk):
    b, s, _ = hq.shape
    nb = s // blk
    fwd = pl.BlockSpec((None, blk, HG_W), lambda bi, j: (bi, j, 0))
    bwd = pl.BlockSpec((None, blk, HG_W), lambda bi, j: (bi, nb - 1 - j, 0))
    return pl.pallas_call(
        functools.partial(_hgrn_kernel, n_chunks=blk // CHUNK),
        grid=(b, nb),
        in_specs=[fwd, fwd, fwd, fwd, bwd, bwd, bwd, bwd, _const_spec(tri.shape)],
        out_specs=[fwd, bwd],
        out_shape=[jax.ShapeDtypeStruct((b, s, HG_W), F32)] * 2,
        scratch_shapes=[pltpu.VMEM((2, HG_HEADS, HG_DV, HG_DK), F32)],
        compiler_params=pltpu.CompilerParams(dimension_semantics=("arbitrary", "arbitrary"),
                                             vmem_limit_bytes=VMEM_LIMIT),
        name="hgrn2",
    )(hq, hv, kf, lff, hq, hv, kb, lfb, tri)


def _out_kernel(x_ref, a_ref, of_ref, ob_ref, gs_ref, p_ref, ghgo_ref, woa_ref, wor_ref, gffn_ref,
                wg_ref, wu_ref, wd_ref, gple_ref, wpg_ref, wpp_ref, out_ref, *, ff_chunk):
    o = of_ref[...] + ob_ref[...]
    parts = []
    for hd in range(HG_HEADS):
        oh = o[:, hd * HG_DV:(hd + 1) * HG_DV]
        parts.append(oh * _rms_scale(jnp.sum(oh * oh, axis=-1, keepdims=True), HG_DV))
    r = jnp.concatenate(parts, axis=-1) * ghgo_ref[...] * gs_ref[...].astype(F32)
    x = (x_ref[...]
         + jnp.dot(a_ref[...], woa_ref[...], preferred_element_type=F32)
         + jnp.dot(r.astype(BF16), wor_ref[...], preferred_element_type=F32))

    h = (x * _rms_scale(jnp.sum(x * x, axis=-1, keepdims=True), D_MODEL) * gffn_ref[...]).astype(BF16)
    ffn = None
    for c in range(D_FF // ff_chunk):
        cs = slice(c * ff_chunk, (c + 1) * ff_chunk)
        gate = jnp.dot(h, wg_ref[:, cs], preferred_element_type=F32)
        up = jnp.dot(h, wu_ref[:, cs], preferred_element_type=F32)
        act = (gate * _sigmoid(gate) * up).astype(BF16)
        down = jnp.dot(act, wd_ref[cs, :], preferred_element_type=F32)
        ffn = down if ffn is None else ffn + down
    x = x + ffn

    h = (x * _rms_scale(jnp.sum(x * x, axis=-1, keepdims=True), D_MODEL) * gple_ref[...]).astype(BF16)
    gate = _sigmoid(jnp.dot(h, wpg_ref[...], preferred_element_type=F32))
    proj = jnp.dot(p_ref[...].astype(BF16), wpp_ref[...], preferred_element_type=F32)
    out_ref[...] = x + gate * proj


def _out_ffn_ple(x2, a, o_f, o_b, gs, p2, ghgo, woa, wor, gffn, wg, wu, wd, gple, wpg, wpp, tm):
    t = x2.shape[0]
    row = lambda w: pl.BlockSpec((tm, w), lambda i: (i, 0))
    consts = (ghgo, woa, wor, gffn, wg, wu, wd, gple, wpg, wpp)
    return pl.pallas_call(
        functools.partial(_out_kernel, ff_chunk=256),
        grid=(t // tm,),
        in_specs=[row(D_MODEL), row(HG_W), row(HG_W), row(HG_W), row(HG_W), row(PLE_DIM)]
                 + [_const_spec(c.shape) for c in consts],
        out_specs=row(D_MODEL),
        out_shape=jax.ShapeDtypeStruct((t, D_MODEL), F32),
        compiler_params=pltpu.CompilerParams(dimension_semantics=("arbitrary",),
                                             vmem_limit_bytes=VMEM_LIMIT),
        name="out_ffn_ple",
    )(x2, a, o_f, o_b, gs, p2, *consts)


def _swap_halves(w):
    half = w.shape[-1] // 2
    return jnp.concatenate([w[..., half:], w[..., :half]], axis=-1)


def _pad_lanes(w):
    return jnp.pad(w, [(0, 0)] * (w.ndim - 1) + [(0, LANES - w.shape[-1])])


def kernel(x, p, positions, g_mix, w_in, g_qa, g_kva, w_qb, w_kvb, g_qn, g_kn, lb_param, g_hgo,
           w_o, g_ffn, w_gate, w_up, w_down, g_ple, w_ple_gate, w_ple_proj):
    bsz, seq, _ = x.shape
    t = bsz * seq
    layer = 0

    n_freq = QK_ROPE // 2
    inv_freq = (ROPE_THETA ** (-np.arange(0, QK_ROPE, 2, dtype=np.float64) / QK_ROPE)).astype(np.float32)
    freq_row = jnp.asarray(np.tile(inv_freq, LANES // n_freq)[None, :])
    pos_dense = jnp.repeat(positions.reshape(-1), n_freq).reshape(t * n_freq // LANES, LANES)
    cos_d, sin_d = _rope_tables(pos_dense, freq_row)
    cos = cos_d.reshape(t, n_freq)
    sin = sin_d.reshape(t, n_freq)
    cos_t = jnp.tile(cos, (1, LANES // n_freq))
    sin_t = jnp.tile(jnp.concatenate([-sin, sin], axis=-1), (1, LANES // QK_ROPE))

    wi = w_in[layer]
    c_q, c_kv, k_r, h5 = (wi[:, :Q_LORA], wi[:, Q_LORA:Q_LORA + KV_LORA],
                          wi[:, Q_LORA + KV_LORA:Q_LORA + KV_LORA + QK_ROPE],
                          wi[:, Q_LORA + KV_LORA + QK_ROPE:])
    w1 = jnp.concatenate([c_q, c_kv, h5, _pad_lanes(k_r), _pad_lanes(_swap_halves(k_r))], axis=1).astype(BF16)

    wq = w_qb[layer].reshape(Q_LORA, MLA_HEADS, QK_HEAD)
    wq_n = wq[:, :, :QK_NOPE].reshape(Q_LORA, -1)
    wq_r = _pad_lanes(wq[:, :, QK_NOPE:]).reshape(Q_LORA, -1)
    wq_rs = _pad_lanes(_swap_halves(wq[:, :, QK_NOPE:])).reshape(Q_LORA, -1)
    wq2 = jnp.concatenate([wq_n, wq_r, wq_rs], axis=1).astype(BF16)

    wkv = w_kvb[layer].reshape(KV_LORA, MLA_HEADS, QK_NOPE + V_HEAD)
    wkv2 = jnp.concatenate([wkv[:, :, :QK_NOPE].reshape(KV_LORA, -1),
                            wkv[:, :, QK_NOPE:].reshape(KV_LORA, -1)], axis=1).astype(BF16)

    def qk_gain(g):
        g_r = g[QK_NOPE:]
        return jnp.stack([g[:QK_NOPE], _pad_lanes(g_r), _pad_lanes(_swap_halves(g_r))]).astype(F32)

    lower = jax.nn.softmax(lb_param.astype(F32), axis=0)[layer]
    row = lambda g: g.reshape(1, -1).astype(F32)

    x2 = x.reshape(t, D_MODEL)
    q, k, v, hq, hv, kf, kb, lff, lfb, gs = _in_proj(
        x2, cos_t, sin_t, row(g_mix[layer]), w1, row(g_qa[layer]), row(g_kva[layer]), wq2, wkv2,
        qk_gain(g_qn[layer]), qk_gain(g_kn[layer]), lower, tm=512)

    a = _attention(q.reshape(bsz, seq, -1), k.reshape(bsz, seq, -1), v.reshape(bsz, seq, -1), tq=512)

    tt = np.arange(CHUNK)
    tri = jnp.asarray(np.stack([tt[:, None] >= tt[None, :], tt[:, None] <= tt[None, :]]), dtype=BF16)
    r3 = lambda z: z.reshape(bsz, seq, HG_W)
    o_f, o_b = _hgrn(r3(hq), r3(hv), r3(kf), r3(kb), r3(lff), r3(lfb), tri, blk=512)

    wo = w_o[layer].astype(BF16)
    out = _out_ffn_ple(
        x2, a.reshape(t, -1), o_f.reshape(t, HG_W), o_b.reshape(t, HG_W), gs, p[layer].reshape(t, PLE_DIM),
        row(g_hgo[layer]), wo[:MLA_HEADS * V_HEAD], wo[MLA_HEADS * V_HEAD:], row(g_ffn[layer]),
        w_gate[layer].astype(BF16), w_up[layer].astype(BF16), w_down[layer].astype(BF16),
        row(g_ple[layer]), w_ple_gate[layer].astype(BF16), w_ple_proj[layer].astype(BF16), tm=256)
    return out.reshape(bsz, seq, D_MODEL)
```

```python
import functools

import jax
import jax.numpy as jnp
import numpy as np
from jax import lax
from jax.experimental import pallas as pl
from jax.experimental.pallas import tpu as pltpu

D_MODEL = 1024
MLA_HEADS = 4
QK_NOPE = 128
QK_ROPE = 64
V_HEAD = 128
Q_LORA = 256
KV_LORA = 256
QK_HEAD = QK_NOPE + QK_ROPE
HG_HEADS = 4
HG_DK = 128
HG_DV = 128
HG_W = HG_HEADS * HG_DK
CHUNK = 64
D_FF = 2816
PLE_DIM = 256
ROPE_THETA = 10000.0
EPS = 1e-6

LANES = 128
QK_PAD = 2 * LANES
VMEM_LIMIT = 56 * 1024 * 1024

BF16 = jnp.bfloat16
F32 = jnp.float32


def _const_spec(shape):
    return pl.BlockSpec(shape, lambda *_: (0,) * len(shape), pipeline_mode=pl.Buffered(1))


def _sigmoid(x):
    return 1.0 / (1.0 + jnp.exp(-x))


def _rms_scale(ss, n):
    return lax.rsqrt(ss * (1.0 / n) + EPS)


def _rope_kernel(pos_ref, freq_ref, cos_ref, sin_ref):
    ang = pos_ref[...].astype(F32) * freq_ref[...]
    cos_ref[...] = jnp.cos(ang)
    sin_ref[...] = jnp.sin(ang)


def _rope_tables(pos_dense, freq_row):
    rows = pos_dense.shape[0]
    blk = 1024
    spec = pl.BlockSpec((blk, LANES), lambda i: (i, 0))
    return pl.pallas_call(
        _rope_kernel,
        grid=(rows // blk,),
        in_specs=[spec, pl.BlockSpec((1, LANES), lambda i: (0, 0))],
        out_specs=[spec, spec],
        out_shape=[jax.ShapeDtypeStruct((rows, LANES), F32)] * 2,
        name="rope_tables",
    )(pos_dense, freq_row)


def _in_proj_kernel(x_ref, cos_ref, sin_ref, gmix_ref, w1_ref, gqa_ref, gkva_ref, wq2_ref, wkv2_ref,
                    gq_ref, gk_ref, lb_ref,
                    q_ref, k_ref, v_ref, hq_ref, hv_ref, kf_ref, kb_ref, lff_ref, lfb_ref, gs_ref):
    x = x_ref[...]
    h = x * _rms_scale(jnp.sum(x * x, axis=-1, keepdims=True), D_MODEL) * gmix_ref[...]
    z = jnp.dot(h.astype(BF16), w1_ref[...], preferred_element_type=F32)

    c_q = z[:, 0:Q_LORA]
    c_kv = z[:, Q_LORA:Q_LORA + KV_LORA]
    o = Q_LORA + KV_LORA
    hq = z[:, o:o + HG_W]
    hff = z[:, o + HG_W:o + 2 * HG_W]
    hfb = z[:, o + 2 * HG_W:o + 3 * HG_W]
    hi = z[:, o + 3 * HG_W:o + 4 * HG_W]
    hg = z[:, o + 4 * HG_W:o + 5 * HG_W]
    o += 5 * HG_W
    k_r = z[:, o:o + LANES]
    k_rs = z[:, o + LANES:o + 2 * LANES]

    cos = cos_ref[...]
    sin = sin_ref[...]

    cqn = c_q * _rms_scale(jnp.sum(c_q * c_q, axis=-1, keepdims=True), Q_LORA) * gqa_ref[...]
    q_all = jnp.dot(cqn.astype(BF16), wq2_ref[...], preferred_element_type=F32)
    gq_n, gq_r, gq_rs = gq_ref[0:1, :], gq_ref[1:2, :], gq_ref[2:3, :]
    scale = QK_HEAD ** -0.5 * np.log2(np.e)
    nh = MLA_HEADS
    for hd in range(nh):
        qn = q_all[:, hd * LANES:(hd + 1) * LANES]
        qr = q_all[:, (nh + hd) * LANES:(nh + hd + 1) * LANES]
        qrs = q_all[:, (2 * nh + hd) * LANES:(2 * nh + hd + 1) * LANES]
        ss = jnp.sum(qn * qn + qr * qr, axis=-1, keepdims=True)
        r = _rms_scale(ss, QK_HEAD) * scale
        q_ref[:, hd * QK_PAD:hd * QK_PAD + LANES] = (qn * gq_n * r).astype(BF16)
        q_ref[:, hd * QK_PAD + LANES:(hd + 1) * QK_PAD] = (
            (qr * gq_r * cos + qrs * gq_rs * sin) * r).astype(BF16)

    ckvn = c_kv * _rms_scale(jnp.sum(c_kv * c_kv, axis=-1, keepdims=True), KV_LORA) * gkva_ref[...]
    kv_all = jnp.dot(ckvn.astype(BF16), wkv2_ref[...], preferred_element_type=F32)
    gk_n, gk_r, gk_rs = gk_ref[0:1, :], gk_ref[1:2, :], gk_ref[2:3, :]
    k_rot = k_r * gk_r * cos + k_rs * gk_rs * sin
    ss_r = jnp.sum(k_r * k_r, axis=-1, keepdims=True)
    for hd in range(nh):
        kn = kv_all[:, hd * LANES:(hd + 1) * LANES]
        ss = jnp.sum(kn * kn, axis=-1, keepdims=True) + ss_r
        r = _rms_scale(ss, QK_HEAD)
        k_ref[:, hd * QK_PAD:hd * QK_PAD + LANES] = (kn * gk_n * r).astype(BF16)
        k_ref[:, hd * QK_PAD + LANES:(hd + 1) * QK_PAD] = (k_rot * r).astype(BF16)
    v_ref[...] = kv_all[:, nh * LANES:].astype(BF16)

    hq_ref[...] = (hq * _sigmoid(hq)).astype(BF16)
    hv_ref[...] = hi.astype(BF16)
    gs_ref[...] = (hg * _sigmoid(hg)).astype(BF16)
    for f_pre, lb, k_out, lf_out in ((hff, lb_ref[0:1, :], kf_ref, lff_ref),
                                     (hfb, lb_ref[1:2, :], kb_ref, lfb_ref)):
        f = lb + (1.0 - lb) * _sigmoid(f_pre)
        k_out[...] = (1.0 - f).astype(BF16)
        lf_out[...] = jnp.log(f)


def _in_proj(x2, cos_t, sin_t, gmix, w1, gqa, gkva, wq2, wkv2, gq, gk, lb, tm):
    t = x2.shape[0]
    row = lambda w: pl.BlockSpec((tm, w), lambda i: (i, 0))
    outs = [(MLA_HEADS * QK_PAD, BF16), (MLA_HEADS * QK_PAD, BF16), (MLA_HEADS * V_HEAD, BF16),
            (HG_W, BF16), (HG_W, BF16), (HG_W, BF16), (HG_W, BF16), (HG_W, F32), (HG_W, F32), (HG_W, BF16)]
    return pl.pallas_call(
        _in_proj_kernel,
        grid=(t // tm,),
        in_specs=[row(D_MODEL), row(LANES), row(LANES),
                  _const_spec(gmix.shape), _const_spec(w1.shape), _const_spec(gqa.shape),
                  _const_spec(gkva.shape), _const_spec(wq2.shape), _const_spec(wkv2.shape),
                  _const_spec(gq.shape), _const_spec(gk.shape), _const_spec(lb.shape)],
        out_specs=[row(w) for w, _ in outs],
        out_shape=[jax.ShapeDtypeStruct((t, w), dt) for w, dt in outs],
        compiler_params=pltpu.CompilerParams(dimension_semantics=("arbitrary",),
                                             vmem_limit_bytes=VMEM_LIMIT),
        name="in_proj",
    )(x2, cos_t, sin_t, gmix, w1, gqa, gkva, wq2, wkv2, gq, gk, lb)


def _attn_kernel(q_ref, k_ref, v_ref, o_ref, s0_ref, s1_ref, *, tq, tk):
    seq = k_ref.shape[0]
    nq, nk, ng = seq // tq, seq // tk, tk // LANES
    nt = (((1,), (1,)), ((), ()))

    def lanes_max(acc, blk):
        for g in range(ng):
            acc = jnp.maximum(acc, blk[:, g * LANES:(g + 1) * LANES])
        return acc

    def lanes_sum(acc, blk):
        for g in range(ng):
            acc = acc + blk[:, g * LANES:(g + 1) * LANES]
        return acc

    def row_bcast(col):
        return jnp.broadcast_to(col, (tq, LANES))

    def q_rows(i):
        start = i * tq
        return pl.ds(start if isinstance(start, int) else pl.multiple_of(start, tq), tq)

    def scores(i_next, s_w):
        q = q_ref[q_rows(i_next), :]
        mrun = jnp.full((tq, LANES), -jnp.inf, F32)
        for kc in range(nk):
            ks = slice(kc * tk, (kc + 1) * tk)
            s_c = lax.dot_general(q, k_ref[ks, :], nt, preferred_element_type=F32)
            s_w[:, ks] = s_c
            mrun = lanes_max(mrun, s_c)
        return row_bcast(jnp.max(mrun, axis=-1, keepdims=True))

    def step(i_next, s_w, i_cur, s_r, m_cur):
        if i_next is not None:
            q = q_ref[q_rows(i_next), :]
            mrun = jnp.full((tq, LANES), -jnp.inf, F32)
        lrun = jnp.zeros((tq, LANES), F32)
        acc = jnp.zeros((tq, V_HEAD), F32)
        for kc in range(nk):
            ks = slice(kc * tk, (kc + 1) * tk)
            if i_next is not None:
                s_c = lax.dot_general(q, k_ref[ks, :], nt, preferred_element_type=F32)
                s_w[:, ks] = s_c
                mrun = lanes_max(mrun, s_c)
            p_parts = [jnp.exp2(s_r[:, kc * tk + g * LANES:kc * tk + (g + 1) * LANES] - m_cur)
                       for g in range(ng)]
            for part in p_parts:
                lrun = lrun + part
            p_c = jnp.concatenate(p_parts, axis=-1).astype(BF16)
            acc = acc + jnp.dot(p_c, v_ref[ks, :], preferred_element_type=F32)
        l = jnp.sum(lrun, axis=-1, keepdims=True)
        o_ref[q_rows(i_cur), :] = (acc / l).astype(o_ref.dtype)
        if i_next is None:
            return None
        return row_bcast(jnp.max(mrun, axis=-1, keepdims=True))

    m0 = scores(0, s0_ref)

    def pair(j, m0):
        i = 2 * j
        m1 = step(i + 1, s1_ref, i, s0_ref, m0)
        return step(i + 2, s0_ref, i + 1, s1_ref, m1)

    m0 = lax.fori_loop(0, nq // 2 - 1, pair, m0)
    m1 = step(nq - 1, s1_ref, nq - 2, s0_ref, m0)
    step(None, None, nq - 1, s1_ref, m1)


def _attention(q, k, v, tq, tk):
    b, s, _ = q.shape
    assert s % (2 * tq) == 0 and s % tk == 0 and tk % LANES == 0
    return pl.pallas_call(
        functools.partial(_attn_kernel, tq=tq, tk=tk),
        grid=(b, MLA_HEADS),
        in_specs=[pl.BlockSpec((None, s, QK_PAD), lambda bi, hi: (bi, 0, hi)),
                  pl.BlockSpec((None, s, QK_PAD), lambda bi, hi: (bi, 0, hi)),
                  pl.BlockSpec((None, s, V_HEAD), lambda bi, hi: (bi, 0, hi))],
        out_specs=pl.BlockSpec((None, s, V_HEAD), lambda bi, hi: (bi, 0, hi)),
        out_shape=jax.ShapeDtypeStruct((b, s, MLA_HEADS * V_HEAD), BF16),
        scratch_shapes=[pltpu.VMEM((tq, s), F32), pltpu.VMEM((tq, s), F32)],
        compiler_params=pltpu.CompilerParams(dimension_semantics=("arbitrary", "arbitrary"),
                                             vmem_limit_bytes=VMEM_LIMIT),
        name="attention",
    )(q, k, v)


def _hgrn_kernel(qf_ref, vf_ref, kf_ref, lff_ref, qb_ref, vb_ref, kb_ref, lfb_ref, tri_ref,
                 of_ref, ob_ref, st_ref, *, n_chunks):
    @pl.when(pl.program_id(1) == 0)
    def _():
        st_ref[...] = jnp.zeros_like(st_ref)

    t_idx = lax.broadcasted_iota(jnp.int32, (CHUNK, CHUNK), 0)
    s_idx = lax.broadcasted_iota(jnp.int32, (CHUNK, CHUNK), 1)
    streams = (
        (qf_ref, vf_ref, kf_ref, lff_ref, of_ref, t_idx >= s_idx, CHUNK // 2 - 1, CHUNK - 1),
        (qb_ref, vb_ref, kb_ref, lfb_ref, ob_ref, t_idx <= s_idx, CHUNK // 2, 0),
    )

    def body(c, carry):
        for d, (q_ref, v_ref, k_ref, lf_ref, o_ref, mask, mid, last) in enumerate(streams):
            cc = c if d == 0 else n_chunks - 1 - c
            rows = pl.ds(pl.multiple_of(cc * CHUNK, CHUNK), CHUNK)
            lf = lf_ref[rows, :]
            lf_hi = lf.astype(BF16)
            lf_lo = (lf - lf_hi.astype(F32)).astype(BF16)
            tri = tri_ref[d]
            b = (jnp.dot(tri, lf_hi, preferred_element_type=F32)
                 + jnp.dot(tri, lf_lo, preferred_element_type=F32))
            b_mid = b[mid:mid + 1, :]
            b_last = b[last:last + 1, :]
            q = q_ref[rows, :].astype(F32)
            k = k_ref[rows, :].astype(F32)
            q_in = q * jnp.exp(b - b_mid)
            k_in = k * jnp.exp(b_mid - b)
            q_st = (q_in * jnp.exp(b_mid)).astype(BF16)
            k_st = (k_in * jnp.exp(b_last - b_mid)).astype(BF16)
            q_in = q_in.astype(BF16)
            k_in = k_in.astype(BF16)
            decay = jnp.exp(b_last)
            v = v_ref[rows, :]
            for hd in range(HG_HEADS):
                sl = slice(hd * HG_DK, (hd + 1) * HG_DK)
                a = lax.dot_general(q_in[:, sl], k_in[:, sl], (((1,), (1,)), ((), ())),
                                    preferred_element_type=F32)
                a = jnp.where(mask, a, 0.0).astype(BF16)
                st = st_ref[d, hd]
                o = jnp.dot(a, v[:, sl], preferred_element_type=F32)
                o += lax.dot_general(q_st[:, sl], st.astype(BF16), (((1,), (1,)), ((), ())),
                                     preferred_element_type=F32)
                o_ref[rows, sl] = o
                kv_t = lax.dot_general(v[:, sl], k_st[:, sl], (((0,), (0,)), ((), ())),
                                       preferred_element_type=F32)
                st_ref[d, hd] = st * decay[:, sl] + kv_t
        return carry

    lax.fori_loop(0, n_chunks, body, 0)


def _hgrn(hq, hv, kf, kb, lff, lfb, tri, blk):
    b, s, _ = hq.shape
    nb = s // blk
    fwd = pl.BlockSpec((None, blk, HG_W), lambda bi, j: (bi, j, 0))
    bwd = pl.BlockSpec((None, blk, HG_W), lambda bi, j: (bi, nb - 1 - j, 0))
    return pl.pallas_call(
        functools.partial(_hgrn_kernel, n_chunks=blk // CHUNK),
        grid=(b, nb),
        in_specs=[fwd, fwd, fwd, fwd, bwd, bwd, bwd, bwd, _const_spec(tri.shape)],
        out_specs=[fwd, bwd],
        out_shape=[jax.ShapeDtypeStruct((b, s, HG_W), F32)] * 2,
        scratch_shapes=[pltpu.VMEM((2, HG_HEADS, HG_DV, HG_DK), F32)],
        compiler_params=pltpu.CompilerParams(dimension_semantics=("arbitrary", "arbitrary"),
                                             vmem_limit_bytes=VMEM_LIMIT),
        name="hgrn2",
    )(hq, hv, kf, lff, hq, hv, kb, lfb, tri)


def _out_kernel(x_ref, a_ref, of_ref, ob_ref, gs_ref, p_ref, ghgo_ref, woa_ref, wor_ref, gffn_ref,
                wg_ref, wu_ref, wd_ref, gple_ref, wpg_ref, wpp_ref, out_ref, *, ff_chunk):
    o = of_ref[...] + ob_ref[...]
    parts = []
    for hd in range(HG_HEADS):
        oh = o[:, hd * HG_DV:(hd + 1) * HG_DV]
        parts.append(oh * _rms_scale(jnp.sum(oh * oh, axis=-1, keepdims=True), HG_DV))
    r = jnp.concatenate(parts, axis=-1) * ghgo_ref[...] * gs_ref[...].astype(F32)
    x = (x_ref[...]
         + jnp.dot(a_ref[...], woa_ref[...], preferred_element_type=F32)
         + jnp.dot(r.astype(BF16), wor_ref[...], preferred_element_type=F32))

    h = (x * _rms_scale(jnp.sum(x * x, axis=-1, keepdims=True), D_MODEL) * gffn_ref[...]).astype(BF16)
    ffn = None
    for c in range(D_FF // ff_chunk):
        cs = slice(c * ff_chunk, (c + 1) * ff_chunk)
        gate = jnp.dot(h, wg_ref[:, cs], preferred_element_type=F32)
        up = jnp.dot(h, wu_ref[:, cs], preferred_element_type=F32)
        act = (gate * _sigmoid(gate) * up).astype(BF16)
        down = jnp.dot(act, wd_ref[cs, :], preferred_element_type=F32)
        ffn = down if ffn is None else ffn + down
    x = x + ffn

    h = (x * _rms_scale(jnp.sum(x * x, axis=-1, keepdims=True), D_MODEL) * gple_ref[...]).astype(BF16)
    gate = _sigmoid(jnp.dot(h, wpg_ref[...], preferred_element_type=F32))
    proj = jnp.dot(p_ref[...].astype(BF16), wpp_ref[...], preferred_element_type=F32)
    out_ref[...] = x + gate * proj


def _out_ffn_ple(x2, a, o_f, o_b, gs, p2, ghgo, woa, wor, gffn, wg, wu, wd, gple, wpg, wpp, tm):
    t = x2.shape[0]
    row = lambda w: pl.BlockSpec((tm, w), lambda i: (i, 0))
    consts = (ghgo, woa, wor, gffn, wg, wu, wd, gple, wpg, wpp)
    return pl.pallas_call(
        functools.partial(_out_kernel, ff_chunk=256),
        grid=(t // tm,),
        in_specs=[row(D_MODEL), row(HG_W), row(HG_W), row(HG_W), row(HG_W), row(PLE_DIM)]
                 + [_const_spec(c.shape) for c in consts],
        out_specs=row(D_MODEL),
        out_shape=jax.ShapeDtypeStruct((t, D_MODEL), F32),
        compiler_params=pltpu.CompilerParams(dimension_semantics=("arbitrary",),
                                             vmem_limit_bytes=VMEM_LIMIT),
        name="out_ffn_ple",
    )(x2, a, o_f, o_b, gs, p2, *consts)


def _swap_halves(w):
    half = w.shape[-1] // 2
    return jnp.concatenate([w[..., half:], w[..., :half]], axis=-1)


def _pad_lanes(w):
    return jnp.pad(w, [(0, 0)] * (w.ndim - 1) + [(0, LANES - w.shape[-1])])


def kernel(x, p, positions, g_mix, w_in, g_qa, g_kva, w_qb, w_kvb, g_qn, g_kn, lb_param, g_hgo,
           w_o, g_ffn, w_gate, w_up, w_down, g_ple, w_ple_gate, w_ple_proj):
    bsz, seq, _ = x.shape
    t = bsz * seq
    layer = 0

    n_freq = QK_ROPE // 2
    inv_freq = (ROPE_THETA ** (-np.arange(0, QK_ROPE, 2, dtype=np.float64) / QK_ROPE)).astype(np.float32)
    freq_row = jnp.asarray(np.tile(inv_freq, LANES // n_freq)[None, :])
    pos_dense = jnp.repeat(positions.reshape(-1), n_freq).reshape(t * n_freq // LANES, LANES)
    cos_d, sin_d = _rope_tables(pos_dense, freq_row)
    cos = cos_d.reshape(t, n_freq)
    sin = sin_d.reshape(t, n_freq)
    cos_t = jnp.tile(cos, (1, LANES // n_freq))
    sin_t = jnp.tile(jnp.concatenate([-sin, sin], axis=-1), (1, LANES // QK_ROPE))

    wi = w_in[layer]
    c_q, c_kv, k_r, h5 = (wi[:, :Q_LORA], wi[:, Q_LORA:Q_LORA + KV_LORA],
                          wi[:, Q_LORA + KV_LORA:Q_LORA + KV_LORA + QK_ROPE],
                          wi[:, Q_LORA + KV_LORA + QK_ROPE:])
    w1 = jnp.concatenate([c_q, c_kv, h5, _pad_lanes(k_r), _pad_lanes(_swap_halves(k_r))], axis=1).astype(BF16)

    wq = w_qb[layer].reshape(Q_LORA, MLA_HEADS, QK_HEAD)
    wq_n = wq[:, :, :QK_NOPE].reshape(Q_LORA, -1)
    wq_r = _pad_lanes(wq[:, :, QK_NOPE:]).reshape(Q_LORA, -1)
    wq_rs = _pad_lanes(_swap_halves(wq[:, :, QK_NOPE:])).reshape(Q_LORA, -1)
    wq2 = jnp.concatenate([wq_n, wq_r, wq_rs], axis=1).astype(BF16)

    wkv = w_kvb[layer].reshape(KV_LORA, MLA_HEADS, QK_NOPE + V_HEAD)
    wkv2 = jnp.concatenate([wkv[:, :, :QK_NOPE].reshape(KV_LORA, -1),
                            wkv[:, :, QK_NOPE:].reshape(KV_LORA, -1)], axis=1).astype(BF16)

    def qk_gain(g):
        g_r = g[QK_NOPE:]
        return jnp.stack([g[:QK_NOPE], _pad_lanes(g_r), _pad_lanes(_swap_halves(g_r))]).astype(F32)

    lower = jax.nn.softmax(lb_param.astype(F32), axis=0)[layer]
    row = lambda g: g.reshape(1, -1).astype(F32)

    x2 = x.reshape(t, D_MODEL)
    q, k, v, hq, hv, kf, kb, lff, lfb, gs = _in_proj(
        x2, cos_t, sin_t, row(g_mix[layer]), w1, row(g_qa[layer]), row(g_kva[layer]), wq2, wkv2,
        qk_gain(g_qn[layer]), qk_gain(g_kn[layer]), lower, tm=512)

    a = _attention(q.reshape(bsz, seq, -1), k.reshape(bsz, seq, -1), v.reshape(bsz, seq, -1), tq=256, tk=512)

    tt = np.arange(CHUNK)
    tri = jnp.asarray(np.stack([tt[:, None] >= tt[None, :], tt[:, None] <= tt[None, :]]), dtype=BF16)
    r3 = lambda z: z.reshape(bsz, seq, HG_W)
    o_f, o_b = _hgrn(r3(hq), r3(hv), r3(kf), r3(kb), r3(lff), r3(lfb), tri, blk=512)

    wo = w_o[layer].astype(BF16)
    out = _out_ffn_ple(
        x2, a.reshape(t, -1), o_f.reshape(t, HG_W), o_b.reshape(t, HG_W), gs, p[layer].reshape(t, PLE_DIM),
        row(g_hgo[layer]), wo[:MLA_HEADS * V_HEAD], wo[MLA_HEADS * V_HEAD:], row(g_ffn[layer]),
        w_gate[layer].astype(BF16), w_up[layer].astype(BF16), w_down[layer].astype(BF16),
        row(g_ple[layer]), w_ple_gate[layer].astype(BF16), w_ple_proj[layer].astype(BF16), tm=256)
    return out.reshape(bsz, seq, D_MODEL)
```

```python
import functools

import jax
import jax.numpy as jnp
import numpy as np
from jax import lax
from jax.experimental import pallas as pl
from jax.experimental.pallas import tpu as pltpu

D_MODEL = 1024
MLA_HEADS = 4
QK_NOPE = 128
QK_ROPE = 64
V_HEAD = 128
Q_LORA = 256
KV_LORA = 256
QK_HEAD = QK_NOPE + QK_ROPE
HG_HEADS = 4
HG_DK = 128
HG_DV = 128
HG_W = HG_HEADS * HG_DK
CHUNK = 64
D_FF = 2816
PLE_DIM = 256
ROPE_THETA = 10000.0
EPS = 1e-6

LANES = 128
QK_PAD = 2 * LANES
VMEM_LIMIT = 56 * 1024 * 1024

BF16 = jnp.bfloat16
F32 = jnp.float32


def _const_spec(shape):
    return pl.BlockSpec(shape, lambda *_: (0,) * len(shape), pipeline_mode=pl.Buffered(1))


def _sigmoid(x):
    return 1.0 / (1.0 + jnp.exp(-x))


def _rms_scale(ss, n):
    return lax.rsqrt(ss * (1.0 / n) + EPS)


def _rope_kernel(pos_ref, freq_ref, cos_ref, sin_ref):
    ang = pos_ref[...].astype(F32) * freq_ref[...]
    cos_ref[...] = jnp.cos(ang)
    sin_ref[...] = jnp.sin(ang)


def _rope_tables(pos_dense, freq_row):
    rows = pos_dense.shape[0]
    blk = 1024
    spec = pl.BlockSpec((blk, LANES), lambda i: (i, 0))
    return pl.pallas_call(
        _rope_kernel,
        grid=(rows // blk,),
        in_specs=[spec, pl.BlockSpec((1, LANES), lambda i: (0, 0))],
        out_specs=[spec, spec],
        out_shape=[jax.ShapeDtypeStruct((rows, LANES), F32)] * 2,
        name="rope_tables",
    )(pos_dense, freq_row)


def _in_proj_kernel(x_ref, cos_ref, sin_ref, gmix_ref, w1_ref, gqa_ref, gkva_ref, wq2_ref, wkv2_ref,
                    gq_ref, gk_ref, lb_ref,
                    q_ref, k_ref, v_ref, hq_ref, hv_ref, kf_ref, kb_ref, lhf_ref, llf_ref, lhb_ref, llb_ref,
                    gs_ref):
    x = x_ref[...]
    h = x * _rms_scale(jnp.sum(x * x, axis=-1, keepdims=True), D_MODEL) * gmix_ref[...]
    z = jnp.dot(h.astype(BF16), w1_ref[...], preferred_element_type=F32)

    c_q = z[:, 0:Q_LORA]
    c_kv = z[:, Q_LORA:Q_LORA + KV_LORA]
    o = Q_LORA + KV_LORA
    hq = z[:, o:o + HG_W]
    hff = z[:, o + HG_W:o + 2 * HG_W]
    hfb = z[:, o + 2 * HG_W:o + 3 * HG_W]
    hi = z[:, o + 3 * HG_W:o + 4 * HG_W]
    hg = z[:, o + 4 * HG_W:o + 5 * HG_W]
    o += 5 * HG_W
    k_r = z[:, o:o + LANES]
    k_rs = z[:, o + LANES:o + 2 * LANES]

    cos = cos_ref[...]
    sin = sin_ref[...]

    cqn = c_q * _rms_scale(jnp.sum(c_q * c_q, axis=-1, keepdims=True), Q_LORA) * gqa_ref[...]
    q_all = jnp.dot(cqn.astype(BF16), wq2_ref[...], preferred_element_type=F32)
    gq_n, gq_r, gq_rs = gq_ref[0:1, :], gq_ref[1:2, :], gq_ref[2:3, :]
    scale = QK_HEAD ** -0.5 * np.log2(np.e)
    nh = MLA_HEADS
    for hd in range(nh):
        qn = q_all[:, hd * LANES:(hd + 1) * LANES]
        qr = q_all[:, (nh + hd) * LANES:(nh + hd + 1) * LANES]
        qrs = q_all[:, (2 * nh + hd) * LANES:(2 * nh + hd + 1) * LANES]
        ss = jnp.sum(qn * qn + qr * qr, axis=-1, keepdims=True)
        r = _rms_scale(ss, QK_HEAD) * scale
        q_ref[:, hd * QK_PAD:hd * QK_PAD + LANES] = (qn * gq_n * r).astype(BF16)
        q_ref[:, hd * QK_PAD + LANES:(hd + 1) * QK_PAD] = (
            (qr * gq_r * cos + qrs * gq_rs * sin) * r).astype(BF16)

    ckvn = c_kv * _rms_scale(jnp.sum(c_kv * c_kv, axis=-1, keepdims=True), KV_LORA) * gkva_ref[...]
    kv_all = jnp.dot(ckvn.astype(BF16), wkv2_ref[...], preferred_element_type=F32)
    gk_n, gk_r, gk_rs = gk_ref[0:1, :], gk_ref[1:2, :], gk_ref[2:3, :]
    k_rot = k_r * gk_r * cos + k_rs * gk_rs * sin
    ss_r = jnp.sum(k_r * k_r, axis=-1, keepdims=True)
    for hd in range(nh):
        kn = kv_all[:, hd * LANES:(hd + 1) * LANES]
        ss = jnp.sum(kn * kn, axis=-1, keepdims=True) + ss_r
        r = _rms_scale(ss, QK_HEAD)
        k_ref[:, hd * QK_PAD:hd * QK_PAD + LANES] = (kn * gk_n * r).astype(BF16)
        k_ref[:, hd * QK_PAD + LANES:(hd + 1) * QK_PAD] = (k_rot * r).astype(BF16)
    v_ref[...] = kv_all[:, nh * LANES:].astype(BF16)

    hq_ref[...] = hq * _sigmoid(hq)
    hv_ref[...] = hi.astype(BF16)
    gs_ref[...] = (hg * _sigmoid(hg)).astype(BF16)
    for f_pre, lb, k_out, hi_out, lo_out in ((hff, lb_ref[0:1, :], kf_ref, lhf_ref, llf_ref),
                                             (hfb, lb_ref[1:2, :], kb_ref, lhb_ref, llb_ref)):
        f = lb + (1.0 - lb) * _sigmoid(f_pre)
        k_out[...] = 1.0 - f
        lf = jnp.log2(f)
        lf_hi = lf.astype(BF16)
        hi_out[...] = lf_hi
        lo_out[...] = (lf - lf_hi.astype(F32)).astype(BF16)


def _in_proj(x2, cos_t, sin_t, gmix, w1, gqa, gkva, wq2, wkv2, gq, gk, lb, tm):
    t = x2.shape[0]
    row = lambda w: pl.BlockSpec((tm, w), lambda i: (i, 0))
    outs = [(MLA_HEADS * QK_PAD, BF16), (MLA_HEADS * QK_PAD, BF16), (MLA_HEADS * V_HEAD, BF16),
            (HG_W, F32), (HG_W, BF16), (HG_W, F32), (HG_W, F32),
            (HG_W, BF16), (HG_W, BF16), (HG_W, BF16), (HG_W, BF16), (HG_W, BF16)]
    return pl.pallas_call(
        _in_proj_kernel,
        grid=(t // tm,),
        in_specs=[row(D_MODEL), row(LANES), row(LANES),
                  _const_spec(gmix.shape), _const_spec(w1.shape), _const_spec(gqa.shape),
                  _const_spec(gkva.shape), _const_spec(wq2.shape), _const_spec(wkv2.shape),
                  _const_spec(gq.shape), _const_spec(gk.shape), _const_spec(lb.shape)],
        out_specs=[row(w) for w, _ in outs],
        out_shape=[jax.ShapeDtypeStruct((t, w), dt) for w, dt in outs],
        compiler_params=pltpu.CompilerParams(dimension_semantics=("arbitrary",),
                                             vmem_limit_bytes=VMEM_LIMIT),
        name="in_proj",
    )(x2, cos_t, sin_t, gmix, w1, gqa, gkva, wq2, wkv2, gq, gk, lb)


def _attn_kernel(q_ref, k_ref, v_ref, o_ref, s0_ref, s1_ref, *, tq, tk):
    seq = k_ref.shape[0]
    nq, nk, ng = seq // tq, seq // tk, tk // LANES
    nt = (((1,), (1,)), ((), ()))

    def lanes_max(acc, blk):
        for g in range(ng):
            acc = jnp.maximum(acc, blk[:, g * LANES:(g + 1) * LANES])
        return acc

    def lanes_sum(acc, blk):
        for g in range(ng):
            acc = acc + blk[:, g * LANES:(g + 1) * LANES]
        return acc

    def row_bcast(col):
        return jnp.broadcast_to(col, (tq, LANES))

    def q_rows(i):
        start = i * tq
        return pl.ds(start if isinstance(start, int) else pl.multiple_of(start, tq), tq)

    def scores(i_next, s_w):
        q = q_ref[q_rows(i_next), :]
        mrun = jnp.full((tq, LANES), -jnp.inf, F32)
        for kc in range(nk):
            ks = slice(kc * tk, (kc + 1) * tk)
            s_c = lax.dot_general(q, k_ref[ks, :], nt, preferred_element_type=F32)
            s_w[:, ks] = s_c
            mrun = lanes_max(mrun, s_c)
        return row_bcast(jnp.max(mrun, axis=-1, keepdims=True))

    def step(i_next, s_w, i_cur, s_r, m_cur):
        if i_next is not None:
            q = q_ref[q_rows(i_next), :]
            mrun = jnp.full((tq, LANES), -jnp.inf, F32)
        lrun = jnp.zeros((tq, LANES), F32)
        acc = jnp.zeros((tq, V_HEAD), F32)
        for kc in range(nk):
            ks = slice(kc * tk, (kc + 1) * tk)
            if i_next is not None:
                s_c = lax.dot_general(q, k_ref[ks, :], nt, preferred_element_type=F32)
                s_w[:, ks] = s_c
                mrun = lanes_max(mrun, s_c)
            p_parts = [jnp.exp2(s_r[:, kc * tk + g * LANES:kc * tk + (g + 1) * LANES] - m_cur)
                       for g in range(ng)]
            for part in p_parts:
                lrun = lrun + part
            p_c = jnp.concatenate(p_parts, axis=-1).astype(BF16)
            acc = acc + jnp.dot(p_c, v_ref[ks, :], preferred_element_type=F32)
        l = jnp.sum(lrun, axis=-1, keepdims=True)
        o_ref[q_rows(i_cur), :] = (acc / l).astype(o_ref.dtype)
        if i_next is None:
            return None
        return row_bcast(jnp.max(mrun, axis=-1, keepdims=True))

    m0 = scores(0, s0_ref)

    def pair(j, m0):
        i = 2 * j
        m1 = step(i + 1, s1_ref, i, s0_ref, m0)
        return step(i + 2, s0_ref, i + 1, s1_ref, m1)

    m0 = lax.fori_loop(0, nq // 2 - 1, pair, m0)
    m1 = step(nq - 1, s1_ref, nq - 2, s0_ref, m0)
    step(None, None, nq - 1, s1_ref, m1)


def _attention(q, k, v, tq, tk):
    b, s, _ = q.shape
    assert s % (2 * tq) == 0 and s % tk == 0 and tk % LANES == 0
    return pl.pallas_call(
        functools.partial(_attn_kernel, tq=tq, tk=tk),
        grid=(b, MLA_HEADS),
        in_specs=[pl.BlockSpec((None, s, QK_PAD), lambda bi, hi: (bi, 0, hi)),
                  pl.BlockSpec((None, s, QK_PAD), lambda bi, hi: (bi, 0, hi)),
                  pl.BlockSpec((None, s, V_HEAD), lambda bi, hi: (bi, 0, hi))],
        out_specs=pl.BlockSpec((None, s, V_HEAD), lambda bi, hi: (bi, 0, hi)),
        out_shape=jax.ShapeDtypeStruct((b, s, MLA_HEADS * V_HEAD), BF16),
        scratch_shapes=[pltpu.VMEM((tq, s), F32), pltpu.VMEM((tq, s), F32)],
        compiler_params=pltpu.CompilerParams(dimension_semantics=("arbitrary", "arbitrary"),
                                             vmem_limit_bytes=VMEM_LIMIT),
        name="attention",
    )(q, k, v)


def _hgrn_kernel(qf_ref, vf_ref, kf_ref, lhf_ref, llf_ref, qb_ref, vb_ref, kb_ref, lhb_ref, llb_ref,
                 tri_ref, of_ref, ob_ref, st_ref, *, n_chunks):
    @pl.when(pl.program_id(1) == 0)
    def _():
        st_ref[...] = jnp.zeros_like(st_ref)

    t_idx = lax.broadcasted_iota(jnp.int32, (CHUNK, CHUNK), 0)
    s_idx = lax.broadcasted_iota(jnp.int32, (CHUNK, CHUNK), 1)
    streams = (
        (qf_ref, vf_ref, kf_ref, (lhf_ref, llf_ref), of_ref, t_idx >= s_idx, CHUNK // 2 - 1, CHUNK - 1),
        (qb_ref, vb_ref, kb_ref, (lhb_ref, llb_ref), ob_ref, t_idx <= s_idx, CHUNK // 2, 0),
    )

    nt = (((1,), (1,)), ((), ()))
    tn = (((0,), (0,)), ((), ()))
    heads = [slice(hd * HG_DK, (hd + 1) * HG_DK) for hd in range(HG_HEADS)]

    def chunk_of(d, i):
        return i if d == 0 else n_chunks - 1 - i

    cum, gated, intra = {}, {}, {}
    state = {(d, hd): st_ref[d, hd] for d in range(2) for hd in range(HG_HEADS)}

    def stage_cumsum(d, i):
        lh_ref, ll_ref = streams[d][3]
        rows = slice(chunk_of(d, i) * CHUNK, (chunk_of(d, i) + 1) * CHUNK)
        hi_lo = jnp.concatenate([lh_ref[rows, :], ll_ref[rows, :]], axis=0)
        cum[d, i] = jnp.dot(tri_ref[d], hi_lo, preferred_element_type=F32)

    def stage_gates(d, i):
        q_ref, _, k_ref, _, _, _, mid, last = streams[d]
        rows = slice(chunk_of(d, i) * CHUNK, (chunk_of(d, i) + 1) * CHUNK)
        b = cum.pop((d, i))
        b_mid = b[mid:mid + 1, :]
        b_last = b[last:last + 1, :]
        x = b - b_mid
        q_in = q_ref[rows, :] * jnp.exp2(x)
        k_in = k_ref[rows, :] * jnp.exp2(-x)
        q_st = (q_in * jnp.exp2(b_mid)).astype(BF16)
        k_st = (k_in * jnp.exp2(b_last - b_mid)).astype(BF16)
        gated[d, i] = (q_in.astype(BF16), k_in.astype(BF16), q_st, k_st, jnp.exp2(b_last))

    def stage_intra(d, i):
        v_ref, mask = streams[d][1], streams[d][5]
        rows = slice(chunk_of(d, i) * CHUNK, (chunk_of(d, i) + 1) * CHUNK)
        q_in, k_in, q_st, k_st, decay = gated.pop((d, i))
        v = v_ref[rows, :]
        a, kv_t = [], []
        for sl in heads:
            a_h = lax.dot_general(q_in[:, sl], k_in[:, sl], nt, preferred_element_type=F32)
            a.append(jnp.where(mask, a_h, 0.0).astype(BF16))
            kv_t.append(lax.dot_general(v[:, sl], k_st[:, sl], tn, preferred_element_type=F32))
        intra[d, i] = (rows, v, q_st, a, kv_t, decay)

    def stage_state(d, i):
        o_ref = streams[d][4]
        rows, v, q_st, a, kv_t, decay = intra.pop((d, i))
        for hd, sl in enumerate(heads):
            st = state[d, hd]
            o = jnp.dot(a[hd], v[:, sl], preferred_element_type=F32)
            o += lax.dot_general(q_st[:, sl], st.astype(BF16), nt, preferred_element_type=F32)
            o_ref[rows, sl] = o
            state[d, hd] = st * decay[:, sl] + kv_t[hd]

    stages = (stage_cumsum, stage_gates, stage_intra, stage_state)
    for t in range(n_chunks + len(stages) - 1):
        for lag, stage in enumerate(stages):
            if 0 <= t - lag < n_chunks:
                for d in range(2):
                    stage(d, t - lag)
    for (d, hd), st in state.items():
        st_ref[d, hd] = st


def _hgrn(hq, hv, kf, kb, lhf, llf, lhb, llb, tri, blk):
    b, s, _ = hq.shape
    nb = s // blk
    fwd = pl.BlockSpec((None, blk, HG_W), lambda bi, j: (bi, j, 0))
    bwd = pl.BlockSpec((None, blk, HG_W), lambda bi, j: (bi, nb - 1 - j, 0))
    return pl.pallas_call(
        functools.partial(_hgrn_kernel, n_chunks=blk // CHUNK),
        grid=(b, nb),
        in_specs=[fwd] * 5 + [bwd] * 5 + [_const_spec(tri.shape)],
        out_specs=[fwd, bwd],
        out_shape=[jax.ShapeDtypeStruct((b, s, HG_W), F32)] * 2,
        scratch_shapes=[pltpu.VMEM((2, HG_HEADS, HG_DV, HG_DK), F32)],
        compiler_params=pltpu.CompilerParams(dimension_semantics=("arbitrary", "arbitrary"),
                                             vmem_limit_bytes=VMEM_LIMIT),
        name="hgrn2",
    )(hq, hv, kf, lhf, llf, hq, hv, kb, lhb, llb, tri)


def _out_kernel(x_ref, a_ref, of_ref, ob_ref, gs_ref, p_ref, ghgo_ref, woa_ref, wor_ref, gffn_ref,
                wg_ref, wu_ref, wd_ref, gple_ref, wpg_ref, wpp_ref, out_ref, *, ff_chunk):
    o = of_ref[...] + ob_ref[...]
    parts = []
    for hd in range(HG_HEADS):
        oh = o[:, hd * HG_DV:(hd + 1) * HG_DV]
        parts.append(oh * _rms_scale(jnp.sum(oh * oh, axis=-1, keepdims=True), HG_DV))
    r = jnp.concatenate(parts, axis=-1) * ghgo_ref[...] * gs_ref[...].astype(F32)
    x = (x_ref[...]
         + jnp.dot(a_ref[...], woa_ref[...], preferred_element_type=F32)
         + jnp.dot(r.astype(BF16), wor_ref[...], preferred_element_type=F32))

    h = (x * _rms_scale(jnp.sum(x * x, axis=-1, keepdims=True), D_MODEL) * gffn_ref[...]).astype(BF16)
    ffn = None
    for c in range(D_FF // ff_chunk):
        cs = slice(c * ff_chunk, (c + 1) * ff_chunk)
        gate = jnp.dot(h, wg_ref[:, cs], preferred_element_type=F32)
        up = jnp.dot(h, wu_ref[:, cs], preferred_element_type=F32)
        act = (gate * _sigmoid(gate) * up).astype(BF16)
        down = jnp.dot(act, wd_ref[cs, :], preferred_element_type=F32)
        ffn = down if ffn is None else ffn + down
    x = x + ffn

    h = (x * _rms_scale(jnp.sum(x * x, axis=-1, keepdims=True), D_MODEL) * gple_ref[...]).astype(BF16)
    gate = _sigmoid(jnp.dot(h, wpg_ref[...], preferred_element_type=F32))
    proj = jnp.dot(p_ref[...].astype(BF16), wpp_ref[...], preferred_element_type=F32)
    out_ref[...] = x + gate * proj


def _out_ffn_ple(x2, a, o_f, o_b, gs, p2, ghgo, woa, wor, gffn, wg, wu, wd, gple, wpg, wpp, tm):
    t = x2.shape[0]
    row = lambda w: pl.BlockSpec((tm, w), lambda i: (i, 0))
    consts = (ghgo, woa, wor, gffn, wg, wu, wd, gple, wpg, wpp)
    return pl.pallas_call(
        functools.partial(_out_kernel, ff_chunk=256),
        grid=(t // tm,),
        in_specs=[row(D_MODEL), row(HG_W), row(HG_W), row(HG_W), row(HG_W), row(PLE_DIM)]
                 + [_const_spec(c.shape) for c in consts],
        out_specs=row(D_MODEL),
        out_shape=jax.ShapeDtypeStruct((t, D_MODEL), F32),
        compiler_params=pltpu.CompilerParams(dimension_semantics=("arbitrary",),
                                             vmem_limit_bytes=VMEM_LIMIT),
        name="out_ffn_ple",
    )(x2, a, o_f, o_b, gs, p2, *consts)


def _swap_halves(w):
    half = w.shape[-1] // 2
    return jnp.concatenate([w[..., half:], w[..., :half]], axis=-1)


def _pad_lanes(w):
    return jnp.pad(w, [(0, 0)] * (w.ndim - 1) + [(0, LANES - w.shape[-1])])


def kernel(x, p, positions, g_mix, w_in, g_qa, g_kva, w_qb, w_kvb, g_qn, g_kn, lb_param, g_hgo,
           w_o, g_ffn, w_gate, w_up, w_down, g_ple, w_ple_gate, w_ple_proj):
    bsz, seq, _ = x.shape
    t = bsz * seq
    layer = 0

    n_freq = QK_ROPE // 2
    inv_freq = (ROPE_THETA ** (-np.arange(0, QK_ROPE, 2, dtype=np.float64) / QK_ROPE)).astype(np.float32)
    freq_row = jnp.asarray(np.tile(inv_freq, LANES // n_freq)[None, :])
    pos_dense = jnp.repeat(positions.reshape(-1), n_freq).reshape(t * n_freq // LANES, LANES)
    cos_d, sin_d = _rope_tables(pos_dense, freq_row)
    cos = cos_d.reshape(t, n_freq)
    sin = sin_d.reshape(t, n_freq)
    cos_t = jnp.tile(cos, (1, LANES // n_freq))
    sin_t = jnp.tile(jnp.concatenate([-sin, sin], axis=-1), (1, LANES // QK_ROPE))

    wi = w_in[layer]
    c_q, c_kv, k_r, h5 = (wi[:, :Q_LORA], wi[:, Q_LORA:Q_LORA + KV_LORA],
                          wi[:, Q_LORA + KV_LORA:Q_LORA + KV_LORA + QK_ROPE],
                          wi[:, Q_LORA + KV_LORA + QK_ROPE:])
    w1 = jnp.concatenate([c_q, c_kv, h5, _pad_lanes(k_r), _pad_lanes(_swap_halves(k_r))], axis=1).astype(BF16)

    wq = w_qb[layer].reshape(Q_LORA, MLA_HEADS, QK_HEAD)
    wq_n = wq[:, :, :QK_NOPE].reshape(Q_LORA, -1)
    wq_r = _pad_lanes(wq[:, :, QK_NOPE:]).reshape(Q_LORA, -1)
    wq_rs = _pad_lanes(_swap_halves(wq[:, :, QK_NOPE:])).reshape(Q_LORA, -1)
    wq2 = jnp.concatenate([wq_n, wq_r, wq_rs], axis=1).astype(BF16)

    wkv = w_kvb[layer].reshape(KV_LORA, MLA_HEADS, QK_NOPE + V_HEAD)
    wkv2 = jnp.concatenate([wkv[:, :, :QK_NOPE].reshape(KV_LORA, -1),
                            wkv[:, :, QK_NOPE:].reshape(KV_LORA, -1)], axis=1).astype(BF16)

    def qk_gain(g):
        g_r = g[QK_NOPE:]
        return jnp.stack([g[:QK_NOPE], _pad_lanes(g_r), _pad_lanes(_swap_halves(g_r))]).astype(F32)

    lower = jax.nn.softmax(lb_param.astype(F32), axis=0)[layer]
    row = lambda g: g.reshape(1, -1).astype(F32)

    x2 = x.reshape(t, D_MODEL)
    q, k, v, hq, hv, kf, kb, lhf, llf, lhb, llb, gs = _in_proj(
        x2, cos_t, sin_t, row(g_mix[layer]), w1, row(g_qa[layer]), row(g_kva[layer]), wq2, wkv2,
        qk_gain(g_qn[layer]), qk_gain(g_kn[layer]), lower, tm=512)

    a = _attention(q.reshape(bsz, seq, -1), k.reshape(bsz, seq, -1), v.reshape(bsz, seq, -1), tq=256, tk=512)

    tt = np.arange(CHUNK)
    tri = np.stack([tt[:, None] >= tt[None, :], tt[:, None] <= tt[None, :]])
    tri = jnp.asarray(np.concatenate([tri, tri], axis=-1), dtype=BF16)
    r3 = lambda z: z.reshape(bsz, seq, HG_W)
    o_f, o_b = _hgrn(r3(hq), r3(hv), r3(kf), r3(kb), r3(lhf), r3(llf), r3(lhb), r3(llb), tri, blk=512)

    wo = w_o[layer].astype(BF16)
    out = _out_ffn_ple(
        x2, a.reshape(t, -1), o_f.reshape(t, HG_W), o_b.reshape(t, HG_W), gs, p[layer].reshape(t, PLE_DIM),
        row(g_hgo[layer]), wo[:MLA_HEADS * V_HEAD], wo[MLA_HEADS * V_HEAD:], row(g_ffn[layer]),
        w_gate[layer].astype(BF16), w_up[layer].astype(BF16), w_down[layer].astype(BF16),
        row(g_ple[layer]), w_ple_gate[layer].astype(BF16), w_ple_proj[layer].astype(BF16), tm=256)
    return out.reshape(bsz, seq, D_MODEL)
```

```python
import functools

import jax
import jax.numpy as jnp
import numpy as np
from jax import lax
from jax.experimental import pallas as pl
from jax.experimental.pallas import tpu as pltpu

D_MODEL = 1024
MLA_HEADS = 4
QK_NOPE = 128
QK_ROPE = 64
V_HEAD = 128
Q_LORA = 256
KV_LORA = 256
QK_HEAD = QK_NOPE + QK_ROPE
HG_HEADS = 4
HG_DK = 128
HG_DV = 128
HG_W = HG_HEADS * HG_DK
CHUNK = 64
D_FF = 2816
PLE_DIM = 256
ROPE_THETA = 10000.0
EPS = 1e-6

LANES = 128
QK_PAD = 2 * LANES
VMEM_LIMIT = 56 * 1024 * 1024

BF16 = jnp.bfloat16
F32 = jnp.float32


def _const_spec(shape):
    return pl.BlockSpec(shape, lambda *_: (0,) * len(shape), pipeline_mode=pl.Buffered(1))


def _sigmoid(x):
    return 1.0 / (1.0 + jnp.exp(-x))


def _rms_scale(ss, n):
    return lax.rsqrt(ss * (1.0 / n) + EPS)


N_FREQ = QK_ROPE // 2
TOK_PER_ROW = LANES // N_FREQ


def _split3(x):
    p1 = x.astype(BF16)
    r1 = x - p1.astype(F32)
    p2 = r1.astype(BF16)
    p3 = (r1 - p2.astype(F32)).astype(BF16)
    return p1, p2, p3


def _rope_kernel(pos_ref, freq_ref, selc_ref, sels_ref, cos_ref, sin_ref):
    rows = pos_ref.shape[0]
    pos = pos_ref[...].astype(F32)
    lane = lax.broadcasted_iota(jnp.int32, (rows, LANES), 1)
    pos_d = jnp.broadcast_to(pos[:, TOK_PER_ROW - 1:TOK_PER_ROW], (rows, LANES))
    for i in range(TOK_PER_ROW - 2, -1, -1):
        pos_d = jnp.where(lane < (i + 1) * N_FREQ, jnp.broadcast_to(pos[:, i:i + 1], (rows, LANES)), pos_d)
    ang = pos_d * freq_ref[...]
    for trig, sel_ref, out_ref in ((jnp.cos(ang), selc_ref, cos_ref), (jnp.sin(ang), sels_ref, sin_ref)):
        pieces = jnp.concatenate(_split3(trig), axis=-1)
        for i in range(TOK_PER_ROW):
            out_ref[pl.ds(i, rows, stride=TOK_PER_ROW), :] = jnp.dot(
                pieces, sel_ref[i], preferred_element_type=F32)


def _rope_select_matrices():
    selc = np.zeros((TOK_PER_ROW, 3 * LANES, LANES), np.float32)
    sels = np.zeros_like(selc)
    j = np.arange(N_FREQ)
    for i in range(TOK_PER_ROW):
        for piece in range(3):
            for m in range(LANES // N_FREQ):
                selc[i, piece * LANES + N_FREQ * i + j, N_FREQ * m + j] = 1.0
                sels[i, piece * LANES + N_FREQ * i + j, N_FREQ * m + j] = 1.0 if m % 2 else -1.0
    return jnp.asarray(selc, BF16), jnp.asarray(sels, BF16)


def _rope_tables(pos4, freq_row):
    rows = pos4.shape[0]
    blk = 1024
    selc, sels = _rope_select_matrices()
    out = pl.BlockSpec((blk * TOK_PER_ROW, LANES), lambda i: (i, 0))
    return pl.pallas_call(
        _rope_kernel,
        grid=(rows // blk,),
        in_specs=[pl.BlockSpec((blk, TOK_PER_ROW), lambda i: (i, 0)), _const_spec(freq_row.shape),
                  _const_spec(selc.shape), _const_spec(sels.shape)],
        out_specs=[out, out],
        out_shape=[jax.ShapeDtypeStruct((rows * TOK_PER_ROW, LANES), F32)] * 2,
        compiler_params=pltpu.CompilerParams(dimension_semantics=("arbitrary",)),
        name="rope_tables",
    )(pos4, freq_row, selc, sels)


def _in_proj_kernel(x_ref, cos_ref, sin_ref, gmix_ref, w1_ref, gqa_ref, gkva_ref, wq2_ref, wkv2_ref,
                    gq_ref, gk_ref, lb_ref,
                    q_ref, k_ref, v_ref, hq_ref, hv_ref, kf_ref, kb_ref, lhf_ref, llf_ref, lhb_ref, llb_ref,
                    gs_ref):
    x = x_ref[...]
    h = x * _rms_scale(jnp.sum(x * x, axis=-1, keepdims=True), D_MODEL) * gmix_ref[...]
    z = jnp.dot(h.astype(BF16), w1_ref[...], preferred_element_type=F32)

    c_q = z[:, 0:Q_LORA]
    c_kv = z[:, Q_LORA:Q_LORA + KV_LORA]
    o = Q_LORA + KV_LORA
    hq = z[:, o:o + HG_W]
    hff = z[:, o + HG_W:o + 2 * HG_W]
    hfb = z[:, o + 2 * HG_W:o + 3 * HG_W]
    hi = z[:, o + 3 * HG_W:o + 4 * HG_W]
    hg = z[:, o + 4 * HG_W:o + 5 * HG_W]
    o += 5 * HG_W
    k_r = z[:, o:o + LANES]
    k_rs = z[:, o + LANES:o + 2 * LANES]

    cos = cos_ref[...]
    sin = sin_ref[...]

    cqn = c_q * _rms_scale(jnp.sum(c_q * c_q, axis=-1, keepdims=True), Q_LORA) * gqa_ref[...]
    q_all = jnp.dot(cqn.astype(BF16), wq2_ref[...], preferred_element_type=F32)
    gq_n, gq_r, gq_rs = gq_ref[0:1, :], gq_ref[1:2, :], gq_ref[2:3, :]
    scale = QK_HEAD ** -0.5 * np.log2(np.e)
    nh = MLA_HEADS
    for hd in range(nh):
        qn = q_all[:, hd * LANES:(hd + 1) * LANES]
        qr = q_all[:, (nh + hd) * LANES:(nh + hd + 1) * LANES]
        qrs = q_all[:, (2 * nh + hd) * LANES:(2 * nh + hd + 1) * LANES]
        ss = jnp.sum(qn * qn + qr * qr, axis=-1, keepdims=True)
        r = _rms_scale(ss, QK_HEAD) * scale
        q_ref[:, hd * QK_PAD:hd * QK_PAD + LANES] = (qn * gq_n * r).astype(BF16)
        q_ref[:, hd * QK_PAD + LANES:(hd + 1) * QK_PAD] = (
            (qr * gq_r * cos + qrs * gq_rs * sin) * r).astype(BF16)

    ckvn = c_kv * _rms_scale(jnp.sum(c_kv * c_kv, axis=-1, keepdims=True), KV_LORA) * gkva_ref[...]
    kv_all = jnp.dot(ckvn.astype(BF16), wkv2_ref[...], preferred_element_type=F32)
    gk_n, gk_r, gk_rs = gk_ref[0:1, :], gk_ref[1:2, :], gk_ref[2:3, :]
    k_rot = k_r * gk_r * cos + k_rs * gk_rs * sin
    ss_r = jnp.sum(k_r * k_r, axis=-1, keepdims=True)
    for hd in range(nh):
        kn = kv_all[:, hd * LANES:(hd + 1) * LANES]
        ss = jnp.sum(kn * kn, axis=-1, keepdims=True) + ss_r
        r = _rms_scale(ss, QK_HEAD)
        k_ref[:, hd * QK_PAD:hd * QK_PAD + LANES] = (kn * gk_n * r).astype(BF16)
        k_ref[:, hd * QK_PAD + LANES:(hd + 1) * QK_PAD] = (k_rot * r).astype(BF16)
    v_ref[...] = kv_all[:, nh * LANES:].astype(BF16)

    hq_ref[...] = hq * _sigmoid(hq)
    hv_ref[...] = hi.astype(BF16)
    gs_ref[...] = (hg * _sigmoid(hg)).astype(BF16)
    for f_pre, lb, k_out, hi_out, lo_out in ((hff, lb_ref[0:1, :], kf_ref, lhf_ref, llf_ref),
                                             (hfb, lb_ref[1:2, :], kb_ref, lhb_ref, llb_ref)):
        f = lb + (1.0 - lb) * _sigmoid(f_pre)
        k_out[...] = 1.0 - f
        lf = jnp.log2(f)
        lf_hi = lf.astype(BF16)
        hi_out[...] = lf_hi
        lo_out[...] = (lf - lf_hi.astype(F32)).astype(BF16)


def _in_proj(x2, cos_t, sin_t, gmix, w1, gqa, gkva, wq2, wkv2, gq, gk, lb, tm):
    t = x2.shape[0]
    row = lambda w: pl.BlockSpec((tm, w), lambda i: (i, 0))
    outs = [(MLA_HEADS * QK_PAD, BF16), (MLA_HEADS * QK_PAD, BF16), (MLA_HEADS * V_HEAD, BF16),
            (HG_W, F32), (HG_W, BF16), (HG_W, F32), (HG_W, F32),
            (HG_W, BF16), (HG_W, BF16), (HG_W, BF16), (HG_W, BF16), (HG_W, BF16)]
    return pl.pallas_call(
        _in_proj_kernel,
        grid=(t // tm,),
        in_specs=[row(D_MODEL), row(LANES), row(LANES),
                  _const_spec(gmix.shape), _const_spec(w1.shape), _const_spec(gqa.shape),
                  _const_spec(gkva.shape), _const_spec(wq2.shape), _const_spec(wkv2.shape),
                  _const_spec(gq.shape), _const_spec(gk.shape), _const_spec(lb.shape)],
        out_specs=[row(w) for w, _ in outs],
        out_shape=[jax.ShapeDtypeStruct((t, w), dt) for w, dt in outs],
        compiler_params=pltpu.CompilerParams(dimension_semantics=("arbitrary",),
                                             vmem_limit_bytes=VMEM_LIMIT),
        name="in_proj",
    )(x2, cos_t, sin_t, gmix, w1, gqa, gkva, wq2, wkv2, gq, gk, lb)


def _attn_kernel(q_ref, k_ref, v_ref, o_ref, s0_ref, s1_ref, *, tq, tk):
    seq = k_ref.shape[0]
    nq, nk, ng = seq // tq, seq // tk, tk // LANES
    nt = (((1,), (1,)), ((), ()))

    def lanes_max(acc, blk):
        for g in range(ng):
            acc = jnp.maximum(acc, blk[:, g * LANES:(g + 1) * LANES])
        return acc

    def lanes_sum(acc, blk):
        for g in range(ng):
            acc = acc + blk[:, g * LANES:(g + 1) * LANES]
        return acc

    def row_bcast(col):
        return jnp.broadcast_to(col, (tq, LANES))

    def q_rows(i):
        start = i * tq
        return pl.ds(start if isinstance(start, int) else pl.multiple_of(start, tq), tq)

    def scores(i_next, s_w):
        q = q_ref[q_rows(i_next), :]
        mrun = jnp.full((tq, LANES), -jnp.inf, F32)
        for kc in range(nk):
            ks = slice(kc * tk, (kc + 1) * tk)
            s_c = lax.dot_general(q, k_ref[ks, :], nt, preferred_element_type=F32)
            s_w[:, ks] = s_c
            mrun = lanes_max(mrun, s_c)
        return row_bcast(jnp.max(mrun, axis=-1, keepdims=True))

    def step(i_next, s_w, i_cur, s_r, m_cur):
        if i_next is not None:
            q = q_ref[q_rows(i_next), :]
            mrun = jnp.full((tq, LANES), -jnp.inf, F32)
        lrun = jnp.zeros((tq, LANES), F32)
        acc = jnp.zeros((tq, V_HEAD), F32)
        for kc in range(nk):
            ks = slice(kc * tk, (kc + 1) * tk)
            if i_next is not None:
                s_c = lax.dot_general(q, k_ref[ks, :], nt, preferred_element_type=F32)
                s_w[:, ks] = s_c
                mrun = lanes_max(mrun, s_c)
            p_parts = [jnp.exp2(s_r[:, kc * tk + g * LANES:kc * tk + (g + 1) * LANES] - m_cur)
                       for g in range(ng)]
            for part in p_parts:
                lrun = lrun + part
            p_c = jnp.concatenate(p_parts, axis=-1).astype(BF16)
            acc = acc + jnp.dot(p_c, v_ref[ks, :], preferred_element_type=F32)
        l = jnp.sum(lrun, axis=-1, keepdims=True)
        o_ref[q_rows(i_cur), :] = (acc / l).astype(o_ref.dtype)
        if i_next is None:
            return None
        return row_bcast(jnp.max(mrun, axis=-1, keepdims=True))

    m0 = scores(0, s0_ref)

    def pair(j, m0):
        i = 2 * j
        m1 = step(i + 1, s1_ref, i, s0_ref, m0)
        return step(i + 2, s0_ref, i + 1, s1_ref, m1)

    m0 = lax.fori_loop(0, nq // 2 - 1, pair, m0)
    m1 = step(nq - 1, s1_ref, nq - 2, s0_ref, m0)
    step(None, None, nq - 1, s1_ref, m1)


def _attention(q, k, v, tq, tk):
    b, s, _ = q.shape
    assert s % (2 * tq) == 0 and s % tk == 0 and tk % LANES == 0
    return pl.pallas_call(
        functools.partial(_attn_kernel, tq=tq, tk=tk),
        grid=(b, MLA_HEADS),
        in_specs=[pl.BlockSpec((None, s, QK_PAD), lambda bi, hi: (bi, 0, hi)),
                  pl.BlockSpec((None, s, QK_PAD), lambda bi, hi: (bi, 0, hi)),
                  pl.BlockSpec((None, s, V_HEAD), lambda bi, hi: (bi, 0, hi))],
        out_specs=pl.BlockSpec((None, s, V_HEAD), lambda bi, hi: (bi, 0, hi)),
        out_shape=jax.ShapeDtypeStruct((b, s, MLA_HEADS * V_HEAD), BF16),
        scratch_shapes=[pltpu.VMEM((tq, s), F32), pltpu.VMEM((tq, s), F32)],
        compiler_params=pltpu.CompilerParams(dimension_semantics=("arbitrary", "arbitrary"),
                                             vmem_limit_bytes=VMEM_LIMIT),
        name="attention",
    )(q, k, v)


def _hgrn_kernel(qf_ref, vf_ref, kf_ref, lhf_ref, llf_ref, qb_ref, vb_ref, kb_ref, lhb_ref, llb_ref,
                 tri_ref, of_ref, ob_ref, st_ref, *, n_chunks):
    @pl.when(pl.program_id(1) == 0)
    def _():
        st_ref[...] = jnp.zeros_like(st_ref)

    t_idx = lax.broadcasted_iota(jnp.int32, (CHUNK, CHUNK), 0)
    s_idx = lax.broadcasted_iota(jnp.int32, (CHUNK, CHUNK), 1)
    streams = (
        (qf_ref, vf_ref, kf_ref, (lhf_ref, llf_ref), of_ref, t_idx >= s_idx, CHUNK // 2 - 1, CHUNK - 1),
        (qb_ref, vb_ref, kb_ref, (lhb_ref, llb_ref), ob_ref, t_idx <= s_idx, CHUNK // 2, 0),
    )

    nt = (((1,), (1,)), ((), ()))
    tn = (((0,), (0,)), ((), ()))
    heads = [slice(hd * HG_DK, (hd + 1) * HG_DK) for hd in range(HG_HEADS)]

    def chunk_of(d, i):
        return i if d == 0 else n_chunks - 1 - i

    cum, gated, intra = {}, {}, {}
    state = {(d, hd): st_ref[d, hd] for d in range(2) for hd in range(HG_HEADS)}

    def stage_cumsum(d, i):
        lh_ref, ll_ref = streams[d][3]
        rows = slice(chunk_of(d, i) * CHUNK, (chunk_of(d, i) + 1) * CHUNK)
        hi_lo = jnp.concatenate([lh_ref[rows, :], ll_ref[rows, :]], axis=0)
        cum[d, i] = jnp.dot(tri_ref[d], hi_lo, preferred_element_type=F32)

    def stage_gates(d, i):
        q_ref, _, k_ref, _, _, _, mid, last = streams[d]
        rows = slice(chunk_of(d, i) * CHUNK, (chunk_of(d, i) + 1) * CHUNK)
        b = cum.pop((d, i))
        b_mid = b[mid:mid + 1, :]
        b_last = b[last:last + 1, :]
        x = b - b_mid
        q_in = q_ref[rows, :] * jnp.exp2(x)
        k_in = k_ref[rows, :] * jnp.exp2(-x)
        q_st = (q_in * jnp.exp2(b_mid)).astype(BF16)
        k_st = (k_in * jnp.exp2(b_last - b_mid)).astype(BF16)
        gated[d, i] = (q_in.astype(BF16), k_in.astype(BF16), q_st, k_st, jnp.exp2(b_last))

    def stage_intra(d, i):
        v_ref, mask = streams[d][1], streams[d][5]
        rows = slice(chunk_of(d, i) * CHUNK, (chunk_of(d, i) + 1) * CHUNK)
        q_in, k_in, q_st, k_st, decay = gated.pop((d, i))
        v = v_ref[rows, :]
        a, kv_t = [], []
        for sl in heads:
            a_h = lax.dot_general(q_in[:, sl], k_in[:, sl], nt, preferred_element_type=F32)
            a.append(jnp.where(mask, a_h, 0.0).astype(BF16))
            kv_t.append(lax.dot_general(v[:, sl], k_st[:, sl], tn, preferred_element_type=F32))
        intra[d, i] = (rows, v, q_st, a, kv_t, decay)

    def stage_state(d, i):
        o_ref = streams[d][4]
        rows, v, q_st, a, kv_t, decay = intra.pop((d, i))
        for hd, sl in enumerate(heads):
            st = state[d, hd]
            o = jnp.dot(a[hd], v[:, sl], preferred_element_type=F32)
            o += lax.dot_general(q_st[:, sl], st.astype(BF16), nt, preferred_element_type=F32)
            o_ref[rows, sl] = o
            state[d, hd] = st * decay[:, sl] + kv_t[hd]

    stages = (stage_cumsum, stage_gates, stage_intra, stage_state)
    for t in range(n_chunks + len(stages) - 1):
        for lag, stage in enumerate(stages):
            if 0 <= t - lag < n_chunks:
                for d in range(2):
                    stage(d, t - lag)
    for (d, hd), st in state.items():
        st_ref[d, hd] = st


def _hgrn(hq, hv, kf, kb, lhf, llf, lhb, llb, tri, blk):
    b, s, _ = hq.shape
    nb = s // blk
    fwd = pl.BlockSpec((None, blk, HG_W), lambda bi, j: (bi, j, 0))
    bwd = pl.BlockSpec((None, blk, HG_W), lambda bi, j: (bi, nb - 1 - j, 0))
    return pl.pallas_call(
        functools.partial(_hgrn_kernel, n_chunks=blk // CHUNK),
        grid=(b, nb),
        in_specs=[fwd] * 5 + [bwd] * 5 + [_const_spec(tri.shape)],
        out_specs=[fwd, bwd],
        out_shape=[jax.ShapeDtypeStruct((b, s, HG_W), F32)] * 2,
        scratch_shapes=[pltpu.VMEM((2, HG_HEADS, HG_DV, HG_DK), F32)],
        compiler_params=pltpu.CompilerParams(dimension_semantics=("arbitrary", "arbitrary"),
                                             vmem_limit_bytes=VMEM_LIMIT),
        name="hgrn2",
    )(hq, hv, kf, lhf, llf, hq, hv, kb, lhb, llb, tri)


def _out_kernel(x_ref, a_ref, of_ref, ob_ref, gs_ref, p_ref, ghgo_ref, woa_ref, wor_ref, gffn_ref,
                wg_ref, wu_ref, wd_ref, gple_ref, wpg_ref, wpp_ref, out_ref, *, ff_chunk):
    o = of_ref[...] + ob_ref[...]
    parts = []
    for hd in range(HG_HEADS):
        oh = o[:, hd * HG_DV:(hd + 1) * HG_DV]
        parts.append(oh * _rms_scale(jnp.sum(oh * oh, axis=-1, keepdims=True), HG_DV))
    r = jnp.concatenate(parts, axis=-1) * ghgo_ref[...] * gs_ref[...].astype(F32)
    x = (x_ref[...]
         + jnp.dot(a_ref[...], woa_ref[...], preferred_element_type=F32)
         + jnp.dot(r.astype(BF16), wor_ref[...], preferred_element_type=F32))

    h = (x * _rms_scale(jnp.sum(x * x, axis=-1, keepdims=True), D_MODEL) * gffn_ref[...]).astype(BF16)
    ffn = None
    for c in range(D_FF // ff_chunk):
        cs = slice(c * ff_chunk, (c + 1) * ff_chunk)
        gate = jnp.dot(h, wg_ref[:, cs], preferred_element_type=F32)
        up = jnp.dot(h, wu_ref[:, cs], preferred_element_type=F32)
        act = (gate * _sigmoid(gate) * up).astype(BF16)
        down = jnp.dot(act, wd_ref[cs, :], preferred_element_type=F32)
        ffn = down if ffn is None else ffn + down
    x = x + ffn

    h = (x * _rms_scale(jnp.sum(x * x, axis=-1, keepdims=True), D_MODEL) * gple_ref[...]).astype(BF16)
    gate = _sigmoid(jnp.dot(h, wpg_ref[...], preferred_element_type=F32))
    proj = jnp.dot(p_ref[...].astype(BF16), wpp_ref[...], preferred_element_type=F32)
    out_ref[...] = x + gate * proj


def _out_ffn_ple(x2, a, o_f, o_b, gs, p2, ghgo, woa, wor, gffn, wg, wu, wd, gple, wpg, wpp, tm):
    t = x2.shape[0]
    row = lambda w: pl.BlockSpec((tm, w), lambda i: (i, 0))
    consts = (ghgo, woa, wor, gffn, wg, wu, wd, gple, wpg, wpp)
    return pl.pallas_call(
        functools.partial(_out_kernel, ff_chunk=256),
        grid=(t // tm,),
        in_specs=[row(D_MODEL), row(HG_W), row(HG_W), row(HG_W), row(HG_W), row(PLE_DIM)]
                 + [_const_spec(c.shape) for c in consts],
        out_specs=row(D_MODEL),
        out_shape=jax.ShapeDtypeStruct((t, D_MODEL), F32),
        compiler_params=pltpu.CompilerParams(dimension_semantics=("arbitrary",),
                                             vmem_limit_bytes=VMEM_LIMIT),
        name="out_ffn_ple",
    )(x2, a, o_f, o_b, gs, p2, *consts)


def _swap_halves(w):
    half = w.shape[-1] // 2
    return jnp.concatenate([w[..., half:], w[..., :half]], axis=-1)


def _pad_lanes(w):
    return jnp.pad(w, [(0, 0)] * (w.ndim - 1) + [(0, LANES - w.shape[-1])])


def kernel(x, p, positions, g_mix, w_in, g_qa, g_kva, w_qb, w_kvb, g_qn, g_kn, lb_param, g_hgo,
           w_o, g_ffn, w_gate, w_up, w_down, g_ple, w_ple_gate, w_ple_proj):
    bsz, seq, _ = x.shape
    t = bsz * seq
    layer = 0

    inv_freq = (ROPE_THETA ** (-np.arange(0, QK_ROPE, 2, dtype=np.float64) / QK_ROPE)).astype(np.float32)
    freq_row = jnp.asarray(np.tile(inv_freq, TOK_PER_ROW)[None, :])
    cos_t, sin_t = _rope_tables(positions.reshape(t // TOK_PER_ROW, TOK_PER_ROW), freq_row)

    wi = w_in[layer]
    c_q, c_kv, k_r, h5 = (wi[:, :Q_LORA], wi[:, Q_LORA:Q_LORA + KV_LORA],
                          wi[:, Q_LORA + KV_LORA:Q_LORA + KV_LORA + QK_ROPE],
                          wi[:, Q_LORA + KV_LORA + QK_ROPE:])
    w1 = jnp.concatenate([c_q, c_kv, h5, _pad_lanes(k_r), _pad_lanes(_swap_halves(k_r))], axis=1).astype(BF16)

    wq = w_qb[layer].reshape(Q_LORA, MLA_HEADS, QK_HEAD)
    wq_n = wq[:, :, :QK_NOPE].reshape(Q_LORA, -1)
    wq_r = _pad_lanes(wq[:, :, QK_NOPE:]).reshape(Q_LORA, -1)
    wq_rs = _pad_lanes(_swap_halves(wq[:, :, QK_NOPE:])).reshape(Q_LORA, -1)
    wq2 = jnp.concatenate([wq_n, wq_r, wq_rs], axis=1).astype(BF16)

    wkv = w_kvb[layer].reshape(KV_LORA, MLA_HEADS, QK_NOPE + V_HEAD)
    wkv2 = jnp.concatenate([wkv[:, :, :QK_NOPE].reshape(KV_LORA, -1),
                            wkv[:, :, QK_NOPE:].reshape(KV_LORA, -1)], axis=1).astype(BF16)

    def qk_gain(g):
        g_r = g[QK_NOPE:]
        return jnp.stack([g[:QK_NOPE], _pad_lanes(g_r), _pad_lanes(_swap_halves(g_r))]).astype(F32)

    lower = jax.nn.softmax(lb_param.astype(F32), axis=0)[layer]
    row = lambda g: g.reshape(1, -1).astype(F32)

    x2 = x.reshape(t, D_MODEL)
    q, k, v, hq, hv, kf, kb, lhf, llf, lhb, llb, gs = _in_proj(
        x2, cos_t, sin_t, row(g_mix[layer]), w1, row(g_qa[layer]), row(g_kva[layer]), wq2, wkv2,
        qk_gain(g_qn[layer]), qk_gain(g_kn[layer]), lower, tm=512)

    a = _attention(q.reshape(bsz, seq, -1), k.reshape(bsz, seq, -1), v.reshape(bsz, seq, -1), tq=256, tk=512)

    tt = np.arange(CHUNK)
    tri = np.stack([tt[:, None] >= tt[None, :], tt[:, None] <= tt[None, :]])
    tri = jnp.asarray(np.concatenate([tri, tri], axis=-1), dtype=BF16)
    r3 = lambda z: z.reshape(bsz, seq, HG_W)
    o_f, o_b = _hgrn(r3(hq), r3(hv), r3(kf), r3(kb), r3(lhf), r3(llf), r3(lhb), r3(llb), tri, blk=512)

    wo = w_o[layer].astype(BF16)
    out = _out_ffn_ple(
        x2, a.reshape(t, -1), o_f.reshape(t, HG_W), o_b.reshape(t, HG_W), gs, p[layer].reshape(t, PLE_DIM),
        row(g_hgo[layer]), wo[:MLA_HEADS * V_HEAD], wo[MLA_HEADS * V_HEAD:], row(g_ffn[layer]),
        w_gate[layer].astype(BF16), w_up[layer].astype(BF16), w_down[layer].astype(BF16),
        row(g_ple[layer]), w_ple_gate[layer].astype(BF16), w_ple_proj[layer].astype(BF16), tm=512)
    return out.reshape(bsz, seq, D_MODEL)
```

```python
import functools

import jax
import jax.numpy as jnp
import numpy as np
from jax import lax
from jax.experimental import pallas as pl
from jax.experimental.pallas import tpu as pltpu

D_MODEL = 1024
MLA_HEADS = 4
QK_NOPE = 128
QK_ROPE = 64
V_HEAD = 128
Q_LORA = 256
KV_LORA = 256
QK_HEAD = QK_NOPE + QK_ROPE
HG_HEADS = 4
HG_DK = 128
HG_DV = 128
HG_W = HG_HEADS * HG_DK
CHUNK = 64
D_FF = 2816
PLE_DIM = 256
ROPE_THETA = 10000.0
EPS = 1e-6

LANES = 128
QK_PAD = 2 * LANES
VMEM_LIMIT = 56 * 1024 * 1024

BF16 = jnp.bfloat16
F32 = jnp.float32


def _const_spec(shape):
    return pl.BlockSpec(shape, lambda *_: (0,) * len(shape), pipeline_mode=pl.Buffered(1))


def _sigmoid(x):
    return 1.0 / (1.0 + jnp.exp(-x))


def _rms_scale(ss, n):
    return lax.rsqrt(ss * (1.0 / n) + EPS)


N_FREQ = QK_ROPE // 2
TOK_PER_ROW = LANES // N_FREQ


def _split3(x):
    p1 = x.astype(BF16)
    r1 = x - p1.astype(F32)
    p2 = r1.astype(BF16)
    p3 = (r1 - p2.astype(F32)).astype(BF16)
    return p1, p2, p3


def _rope_kernel(pos_ref, freq_ref, selc_ref, sels_ref, cos_ref, sin_ref):
    rows = pos_ref.shape[0]
    pos = pos_ref[...].astype(F32)
    lane = lax.broadcasted_iota(jnp.int32, (rows, LANES), 1)
    pos_d = jnp.broadcast_to(pos[:, TOK_PER_ROW - 1:TOK_PER_ROW], (rows, LANES))
    for i in range(TOK_PER_ROW - 2, -1, -1):
        pos_d = jnp.where(lane < (i + 1) * N_FREQ, jnp.broadcast_to(pos[:, i:i + 1], (rows, LANES)), pos_d)
    ang = pos_d * freq_ref[...]
    for trig, sel_ref, out_ref in ((jnp.cos(ang), selc_ref, cos_ref), (jnp.sin(ang), sels_ref, sin_ref)):
        pieces = jnp.concatenate(_split3(trig), axis=-1)
        for i in range(TOK_PER_ROW):
            out_ref[pl.ds(i, rows, stride=TOK_PER_ROW), :] = jnp.dot(
                pieces, sel_ref[i], preferred_element_type=F32)


def _rope_select_matrices():
    selc = np.zeros((TOK_PER_ROW, 3 * LANES, LANES), np.float32)
    sels = np.zeros_like(selc)
    j = np.arange(N_FREQ)
    for i in range(TOK_PER_ROW):
        for piece in range(3):
            for m in range(LANES // N_FREQ):
                selc[i, piece * LANES + N_FREQ * i + j, N_FREQ * m + j] = 1.0
                sels[i, piece * LANES + N_FREQ * i + j, N_FREQ * m + j] = 1.0 if m % 2 else -1.0
    return jnp.asarray(selc, BF16), jnp.asarray(sels, BF16)


def _rope_tables(pos4, freq_row):
    rows = pos4.shape[0]
    blk = 1024
    selc, sels = _rope_select_matrices()
    out = pl.BlockSpec((blk * TOK_PER_ROW, LANES), lambda i: (i, 0))
    return pl.pallas_call(
        _rope_kernel,
        grid=(rows // blk,),
        in_specs=[pl.BlockSpec((blk, TOK_PER_ROW), lambda i: (i, 0)), _const_spec(freq_row.shape),
                  _const_spec(selc.shape), _const_spec(sels.shape)],
        out_specs=[out, out],
        out_shape=[jax.ShapeDtypeStruct((rows * TOK_PER_ROW, LANES), F32)] * 2,
        compiler_params=pltpu.CompilerParams(dimension_semantics=("arbitrary",)),
        name="rope_tables",
    )(pos4, freq_row, selc, sels)


def _in_proj_kernel(x_ref, cos_ref, sin_ref, gmix_ref, w1_ref, gqa_ref, gkva_ref, wq2_ref, wkv2_ref,
                    gq_ref, gk_ref, lb_ref,
                    q_ref, k_ref, v_ref, hq_ref, hv_ref, kf_ref, kb_ref, lhf_ref, llf_ref, lhb_ref, llb_ref,
                    gs_ref):
    x = x_ref[...]
    h = x * _rms_scale(jnp.sum(x * x, axis=-1, keepdims=True), D_MODEL) * gmix_ref[...]
    z = jnp.dot(h.astype(BF16), w1_ref[...], preferred_element_type=F32)

    c_q = z[:, 0:Q_LORA]
    c_kv = z[:, Q_LORA:Q_LORA + KV_LORA]
    o = Q_LORA + KV_LORA
    hq = z[:, o:o + HG_W]
    hff = z[:, o + HG_W:o + 2 * HG_W]
    hfb = z[:, o + 2 * HG_W:o + 3 * HG_W]
    hi = z[:, o + 3 * HG_W:o + 4 * HG_W]
    hg = z[:, o + 4 * HG_W:o + 5 * HG_W]
    o += 5 * HG_W
    k_r = z[:, o:o + LANES]
    k_rs = z[:, o + LANES:o + 2 * LANES]

    cos = cos_ref[...]
    sin = sin_ref[...]

    cqn = c_q * _rms_scale(jnp.sum(c_q * c_q, axis=-1, keepdims=True), Q_LORA) * gqa_ref[...]
    q_all = jnp.dot(cqn.astype(BF16), wq2_ref[...], preferred_element_type=F32)
    gq_n, gq_r, gq_rs = gq_ref[0:1, :], gq_ref[1:2, :], gq_ref[2:3, :]
    scale = QK_HEAD ** -0.5 * np.log2(np.e)
    nh = MLA_HEADS
    for hd in range(nh):
        qn = q_all[:, hd * LANES:(hd + 1) * LANES]
        qr = q_all[:, (nh + hd) * LANES:(nh + hd + 1) * LANES]
        qrs = q_all[:, (2 * nh + hd) * LANES:(2 * nh + hd + 1) * LANES]
        ss = jnp.sum(qn * qn + qr * qr, axis=-1, keepdims=True)
        r = _rms_scale(ss, QK_HEAD) * scale
        q_ref[:, hd * QK_PAD:hd * QK_PAD + LANES] = (qn * gq_n * r).astype(BF16)
        q_ref[:, hd * QK_PAD + LANES:(hd + 1) * QK_PAD] = (
            (qr * gq_r * cos + qrs * gq_rs * sin) * r).astype(BF16)

    ckvn = c_kv * _rms_scale(jnp.sum(c_kv * c_kv, axis=-1, keepdims=True), KV_LORA) * gkva_ref[...]
    kv_all = jnp.dot(ckvn.astype(BF16), wkv2_ref[...], preferred_element_type=F32)
    gk_n, gk_r, gk_rs = gk_ref[0:1, :], gk_ref[1:2, :], gk_ref[2:3, :]
    k_rot = k_r * gk_r * cos + k_rs * gk_rs * sin
    ss_r = jnp.sum(k_r * k_r, axis=-1, keepdims=True)
    for hd in range(nh):
        kn = kv_all[:, hd * LANES:(hd + 1) * LANES]
        ss = jnp.sum(kn * kn, axis=-1, keepdims=True) + ss_r
        r = _rms_scale(ss, QK_HEAD)
        k_ref[:, hd * QK_PAD:hd * QK_PAD + LANES] = (kn * gk_n * r).astype(BF16)
        k_ref[:, hd * QK_PAD + LANES:(hd + 1) * QK_PAD] = (k_rot * r).astype(BF16)
    v_ref[...] = kv_all[:, nh * LANES:].astype(BF16)

    hq_ref[...] = hq * _sigmoid(hq)
    hv_ref[...] = hi.astype(BF16)
    gs_ref[...] = (hg * _sigmoid(hg)).astype(BF16)
    for f_pre, lb, k_out, hi_out, lo_out in ((hff, lb_ref[0:1, :], kf_ref, lhf_ref, llf_ref),
                                             (hfb, lb_ref[1:2, :], kb_ref, lhb_ref, llb_ref)):
        f = lb + (1.0 - lb) * _sigmoid(f_pre)
        k_out[...] = 1.0 - f
        lf = jnp.log2(f)
        lf_hi = lf.astype(BF16)
        hi_out[...] = lf_hi
        lo_out[...] = (lf - lf_hi.astype(F32)).astype(BF16)


def _in_proj(x2, cos_t, sin_t, gmix, w1, gqa, gkva, wq2, wkv2, gq, gk, lb, tm):
    t = x2.shape[0]
    row = lambda w: pl.BlockSpec((tm, w), lambda i: (i, 0))
    outs = [(MLA_HEADS * QK_PAD, BF16), (MLA_HEADS * QK_PAD, BF16), (MLA_HEADS * V_HEAD, BF16),
            (HG_W, F32), (HG_W, BF16), (HG_W, F32), (HG_W, F32),
            (HG_W, BF16), (HG_W, BF16), (HG_W, BF16), (HG_W, BF16), (HG_W, BF16)]
    return pl.pallas_call(
        _in_proj_kernel,
        grid=(t // tm,),
        in_specs=[row(D_MODEL), row(LANES), row(LANES),
                  _const_spec(gmix.shape), _const_spec(w1.shape), _const_spec(gqa.shape),
                  _const_spec(gkva.shape), _const_spec(wq2.shape), _const_spec(wkv2.shape),
                  _const_spec(gq.shape), _const_spec(gk.shape), _const_spec(lb.shape)],
        out_specs=[row(w) for w, _ in outs],
        out_shape=[jax.ShapeDtypeStruct((t, w), dt) for w, dt in outs],
        compiler_params=pltpu.CompilerParams(dimension_semantics=("arbitrary",),
                                             vmem_limit_bytes=VMEM_LIMIT),
        name="in_proj",
    )(x2, cos_t, sin_t, gmix, w1, gqa, gkva, wq2, wkv2, gq, gk, lb)


def _attn_kernel(q_ref, k_ref, v_ref, o_ref, s0_ref, s1_ref, vt_ref, *, tq, tk):
    seq = k_ref.shape[0]
    nq, nk = seq // tq, seq // tk
    nt = (((1,), (1,)), ((), ()))
    sub = 8

    def fold(x, op):
        return op(x.reshape(tk // sub, sub, tq), axis=0)

    def q_rows(i):
        start = i * tq
        return pl.ds(start if isinstance(start, int) else pl.multiple_of(start, tq), tq)

    vt_ref[...] = v_ref[...].astype(F32).T.astype(BF16)

    def step(i_next, s_w, i_cur, s_r, m_cur):
        if i_next is not None:
            q = q_ref[q_rows(i_next), :]
            mrun = jnp.full((sub, tq), -jnp.inf, F32)
        if i_cur is not None:
            lrun = jnp.zeros((sub, tq), F32)
            acc = jnp.zeros((V_HEAD, tq), F32)
        for kc in range(nk):
            ks = slice(kc * tk, (kc + 1) * tk)
            if i_next is not None:
                s_c = lax.dot_general(k_ref[ks, :], q, nt, preferred_element_type=F32)
                s_w[ks, :] = s_c
                mrun = jnp.maximum(mrun, fold(s_c, jnp.max))
            if i_cur is not None:
                p_c = jnp.exp2(s_r[ks, :].reshape(tk // sub, sub, tq) - m_cur[None]).reshape(tk, tq)
                lrun = lrun + fold(p_c, jnp.sum)
                acc = acc + jnp.dot(vt_ref[:, ks], p_c.astype(BF16), preferred_element_type=F32)
        if i_cur is not None:
            l = jnp.sum(lrun, axis=0, keepdims=True)
            o_ref[q_rows(i_cur), :] = (acc / l).T.astype(o_ref.dtype)
        if i_next is None:
            return None
        return jnp.broadcast_to(jnp.max(mrun, axis=0, keepdims=True), (sub, tq))

    m0 = step(0, s0_ref, None, None, None)

    def pair(j, m0):
        i = 2 * j
        m1 = step(i + 1, s1_ref, i, s0_ref, m0)
        return step(i + 2, s0_ref, i + 1, s1_ref, m1)

    m0 = lax.fori_loop(0, nq // 2 - 1, pair, m0)
    m1 = step(nq - 1, s1_ref, nq - 2, s0_ref, m0)
    step(None, None, nq - 1, s1_ref, m1)


def _attention(q, k, v, tq, tk):
    b, s, _ = q.shape
    assert s % (2 * tq) == 0 and s % tk == 0 and tk % LANES == 0
    return pl.pallas_call(
        functools.partial(_attn_kernel, tq=tq, tk=tk),
        grid=(b, MLA_HEADS),
        in_specs=[pl.BlockSpec((None, s, QK_PAD), lambda bi, hi: (bi, 0, hi)),
                  pl.BlockSpec((None, s, QK_PAD), lambda bi, hi: (bi, 0, hi)),
                  pl.BlockSpec((None, s, V_HEAD), lambda bi, hi: (bi, 0, hi))],
        out_specs=pl.BlockSpec((None, s, V_HEAD), lambda bi, hi: (bi, 0, hi)),
        out_shape=jax.ShapeDtypeStruct((b, s, MLA_HEADS * V_HEAD), BF16),
        scratch_shapes=[pltpu.VMEM((s, tq), F32), pltpu.VMEM((s, tq), F32),
                        pltpu.VMEM((V_HEAD, s), BF16)],
        compiler_params=pltpu.CompilerParams(dimension_semantics=("arbitrary", "arbitrary"),
                                             vmem_limit_bytes=VMEM_LIMIT),
        name="attention",
    )(q, k, v)


def _hgrn_kernel(qf_ref, vf_ref, kf_ref, lhf_ref, llf_ref, qb_ref, vb_ref, kb_ref, lhb_ref, llb_ref,
                 tri_ref, of_ref, ob_ref, st_ref, *, n_chunks):
    @pl.when(pl.program_id(1) == 0)
    def _():
        st_ref[...] = jnp.zeros_like(st_ref)

    t_idx = lax.broadcasted_iota(jnp.int32, (CHUNK, CHUNK), 0)
    s_idx = lax.broadcasted_iota(jnp.int32, (CHUNK, CHUNK), 1)
    streams = (
        (qf_ref, vf_ref, kf_ref, (lhf_ref, llf_ref), of_ref, t_idx >= s_idx, CHUNK // 2 - 1, CHUNK - 1),
        (qb_ref, vb_ref, kb_ref, (lhb_ref, llb_ref), ob_ref, t_idx <= s_idx, CHUNK // 2, 0),
    )

    nt = (((1,), (1,)), ((), ()))
    tn = (((0,), (0,)), ((), ()))
    heads = [slice(hd * HG_DK, (hd + 1) * HG_DK) for hd in range(HG_HEADS)]

    def chunk_of(d, i):
        return i if d == 0 else n_chunks - 1 - i

    cum, gated, intra = {}, {}, {}
    state = {(d, hd): st_ref[d, hd] for d in range(2) for hd in range(HG_HEADS)}

    def stage_cumsum(d, i):
        lh_ref, ll_ref = streams[d][3]
        rows = slice(chunk_of(d, i) * CHUNK, (chunk_of(d, i) + 1) * CHUNK)
        hi_lo = jnp.concatenate([lh_ref[rows, :], ll_ref[rows, :]], axis=0)
        cum[d, i] = jnp.dot(tri_ref[d], hi_lo, preferred_element_type=F32)

    def stage_gates(d, i):
        q_ref, _, k_ref, _, _, _, mid, last = streams[d]
        rows = slice(chunk_of(d, i) * CHUNK, (chunk_of(d, i) + 1) * CHUNK)
        b = cum.pop((d, i))
        b_mid = b[mid:mid + 1, :]
        b_last = b[last:last + 1, :]
        x = b - b_mid
        q_in = q_ref[rows, :] * jnp.exp2(x)
        k_in = k_ref[rows, :] * jnp.exp2(-x)
        q_st = (q_in * jnp.exp2(b_mid)).astype(BF16)
        k_st = (k_in * jnp.exp2(b_last - b_mid)).astype(BF16)
        gated[d, i] = (q_in.astype(BF16), k_in.astype(BF16), q_st, k_st, jnp.exp2(b_last))

    def stage_intra(d, i):
        v_ref, mask = streams[d][1], streams[d][5]
        rows = slice(chunk_of(d, i) * CHUNK, (chunk_of(d, i) + 1) * CHUNK)
        q_in, k_in, q_st, k_st, decay = gated.pop((d, i))
        v = v_ref[rows, :]
        a, kv_t = [], []
        for sl in heads:
            a_h = lax.dot_general(q_in[:, sl], k_in[:, sl], nt, preferred_element_type=F32)
            a.append(jnp.where(mask, a_h, 0.0).astype(BF16))
            kv_t.append(lax.dot_general(v[:, sl], k_st[:, sl], tn, preferred_element_type=F32))
        intra[d, i] = (rows, v, q_st, a, kv_t, decay)

    def stage_state(d, i):
        o_ref = streams[d][4]
        rows, v, q_st, a, kv_t, decay = intra.pop((d, i))
        for hd, sl in enumerate(heads):
            st = state[d, hd]
            o = jnp.dot(a[hd], v[:, sl], preferred_element_type=F32)
            o += lax.dot_general(q_st[:, sl], st.astype(BF16), nt, preferred_element_type=F32)
            o_ref[rows, sl] = o
            state[d, hd] = st * decay[:, sl] + kv_t[hd]

    stages = (stage_cumsum, stage_gates, stage_intra, stage_state)
    for t in range(n_chunks + len(stages) - 1):
        for lag, stage in enumerate(stages):
            if 0 <= t - lag < n_chunks:
                for d in range(2):
                    stage(d, t - lag)
    for (d, hd), st in state.items():
        st_ref[d, hd] = st


def _hgrn(hq, hv, kf, kb, lhf, llf, lhb, llb, tri, blk):
    b, s, _ = hq.shape
    nb = s // blk
    fwd = pl.BlockSpec((None, blk, HG_W), lambda bi, j: (bi, j, 0))
    bwd = pl.BlockSpec((None, blk, HG_W), lambda bi, j: (bi, nb - 1 - j, 0))
    return pl.pallas_call(
        functools.partial(_hgrn_kernel, n_chunks=blk // CHUNK),
        grid=(b, nb),
        in_specs=[fwd] * 5 + [bwd] * 5 + [_const_spec(tri.shape)],
        out_specs=[fwd, bwd],
        out_shape=[jax.ShapeDtypeStruct((b, s, HG_W), F32)] * 2,
        scratch_shapes=[pltpu.VMEM((2, HG_HEADS, HG_DV, HG_DK), F32)],
        compiler_params=pltpu.CompilerParams(dimension_semantics=("arbitrary", "arbitrary"),
                                             vmem_limit_bytes=VMEM_LIMIT),
        name="hgrn2",
    )(hq, hv, kf, lhf, llf, hq, hv, kb, lhb, llb, tri)


def _out_kernel(x_ref, a_ref, of_ref, ob_ref, gs_ref, p_ref, ghgo_ref, woa_ref, wor_ref, gffn_ref,
                wg_ref, wu_ref, wd_ref, gple_ref, wpg_ref, wpp_ref, out_ref, *, ff_chunk):
    o = of_ref[...] + ob_ref[...]
    parts = []
    for hd in range(HG_HEADS):
        oh = o[:, hd * HG_DV:(hd + 1) * HG_DV]
        parts.append(oh * _rms_scale(jnp.sum(oh * oh, axis=-1, keepdims=True), HG_DV))
    r = jnp.concatenate(parts, axis=-1) * ghgo_ref[...] * gs_ref[...].astype(F32)
    x = (x_ref[...]
         + jnp.dot(a_ref[...], woa_ref[...], preferred_element_type=F32)
         + jnp.dot(r.astype(BF16), wor_ref[...], preferred_element_type=F32))

    h = (x * _rms_scale(jnp.sum(x * x, axis=-1, keepdims=True), D_MODEL) * gffn_ref[...]).astype(BF16)
    ffn = None
    for c in range(D_FF // ff_chunk):
        cs = slice(c * ff_chunk, (c + 1) * ff_chunk)
        gate = jnp.dot(h, wg_ref[:, cs], preferred_element_type=F32)
        up = jnp.dot(h, wu_ref[:, cs], preferred_element_type=F32)
        act = (gate * _sigmoid(gate) * up).astype(BF16)
        down = jnp.dot(act, wd_ref[cs, :], preferred_element_type=F32)
        ffn = down if ffn is None else ffn + down
    x = x + ffn

    h = (x * _rms_scale(jnp.sum(x * x, axis=-1, keepdims=True), D_MODEL) * gple_ref[...]).astype(BF16)
    gate = _sigmoid(jnp.dot(h, wpg_ref[...], preferred_element_type=F32))
    proj = jnp.dot(p_ref[...].astype(BF16), wpp_ref[...], preferred_element_type=F32)
    out_ref[...] = x + gate * proj


def _out_ffn_ple(x2, a, o_f, o_b, gs, p2, ghgo, woa, wor, gffn, wg, wu, wd, gple, wpg, wpp, tm):
    t = x2.shape[0]
    row = lambda w: pl.BlockSpec((tm, w), lambda i: (i, 0))
    consts = (ghgo, woa, wor, gffn, wg, wu, wd, gple, wpg, wpp)
    return pl.pallas_call(
        functools.partial(_out_kernel, ff_chunk=256),
        grid=(t // tm,),
        in_specs=[row(D_MODEL), row(HG_W), row(HG_W), row(HG_W), row(HG_W), row(PLE_DIM)]
                 + [_const_spec(c.shape) for c in consts],
        out_specs=row(D_MODEL),
        out_shape=jax.ShapeDtypeStruct((t, D_MODEL), F32),
        compiler_params=pltpu.CompilerParams(dimension_semantics=("arbitrary",),
                                             vmem_limit_bytes=VMEM_LIMIT),
        name="out_ffn_ple",
    )(x2, a, o_f, o_b, gs, p2, *consts)


def _swap_halves(w):
    half = w.shape[-1] // 2
    return jnp.concatenate([w[..., half:], w[..., :half]], axis=-1)


def _pad_lanes(w):
    return jnp.pad(w, [(0, 0)] * (w.ndim - 1) + [(0, LANES - w.shape[-1])])


def kernel(x, p, positions, g_mix, w_in, g_qa, g_kva, w_qb, w_kvb, g_qn, g_kn, lb_param, g_hgo,
           w_o, g_ffn, w_gate, w_up, w_down, g_ple, w_ple_gate, w_ple_proj):
    bsz, seq, _ = x.shape
    t = bsz * seq
    layer = 0

    inv_freq = (ROPE_THETA ** (-np.arange(0, QK_ROPE, 2, dtype=np.float64) / QK_ROPE)).astype(np.float32)
    freq_row = jnp.asarray(np.tile(inv_freq, TOK_PER_ROW)[None, :])
    cos_t, sin_t = _rope_tables(positions.reshape(t // TOK_PER_ROW, TOK_PER_ROW), freq_row)

    wi = w_in[layer]
    c_q, c_kv, k_r, h5 = (wi[:, :Q_LORA], wi[:, Q_LORA:Q_LORA + KV_LORA],
                          wi[:, Q_LORA + KV_LORA:Q_LORA + KV_LORA + QK_ROPE],
                          wi[:, Q_LORA + KV_LORA + QK_ROPE:])
    w1 = jnp.concatenate([c_q, c_kv, h5, _pad_lanes(k_r), _pad_lanes(_swap_halves(k_r))], axis=1).astype(BF16)

    wq = w_qb[layer].reshape(Q_LORA, MLA_HEADS, QK_HEAD)
    wq_n = wq[:, :, :QK_NOPE].reshape(Q_LORA, -1)
    wq_r = _pad_lanes(wq[:, :, QK_NOPE:]).reshape(Q_LORA, -1)
    wq_rs = _pad_lanes(_swap_halves(wq[:, :, QK_NOPE:])).reshape(Q_LORA, -1)
    wq2 = jnp.concatenate([wq_n, wq_r, wq_rs], axis=1).astype(BF16)

    wkv = w_kvb[layer].reshape(KV_LORA, MLA_HEADS, QK_NOPE + V_HEAD)
    wkv2 = jnp.concatenate([wkv[:, :, :QK_NOPE].reshape(KV_LORA, -1),
                            wkv[:, :, QK_NOPE:].reshape(KV_LORA, -1)], axis=1).astype(BF16)

    def qk_gain(g):
        g_r = g[QK_NOPE:]
        return jnp.stack([g[:QK_NOPE], _pad_lanes(g_r), _pad_lanes(_swap_halves(g_r))]).astype(F32)

    lower = jax.nn.softmax(lb_param.astype(F32), axis=0)[layer]
    row = lambda g: g.reshape(1, -1).astype(F32)

    x2 = x.reshape(t, D_MODEL)
    q, k, v, hq, hv, kf, kb, lhf, llf, lhb, llb, gs = _in_proj(
        x2, cos_t, sin_t, row(g_mix[layer]), w1, row(g_qa[layer]), row(g_kva[layer]), wq2, wkv2,
        qk_gain(g_qn[layer]), qk_gain(g_kn[layer]), lower, tm=512)

    a = _attention(q.reshape(bsz, seq, -1), k.reshape(bsz, seq, -1), v.reshape(bsz, seq, -1), tq=256, tk=512)

    tt = np.arange(CHUNK)
    tri = np.stack([tt[:, None] >= tt[None, :], tt[:, None] <= tt[None, :]])
    tri = jnp.asarray(np.concatenate([tri, tri], axis=-1), dtype=BF16)
    r3 = lambda z: z.reshape(bsz, seq, HG_W)
    o_f, o_b = _hgrn(r3(hq), r3(hv), r3(kf), r3(kb), r3(lhf), r3(llf), r3(lhb), r3(llb), tri, blk=512)

    wo = w_o[layer].astype(BF16)
    out = _out_ffn_ple(
        x2, a.reshape(t, -1), o_f.reshape(t, HG_W), o_b.reshape(t, HG_W), gs, p[layer].reshape(t, PLE_DIM),
        row(g_hgo[layer]), wo[:MLA_HEADS * V_HEAD], wo[MLA_HEADS * V_HEAD:], row(g_ffn[layer]),
        w_gate[layer].astype(BF16), w_up[layer].astype(BF16), w_down[layer].astype(BF16),
        row(g_ple[layer]), w_ple_gate[layer].astype(BF16), w_ple_proj[layer].astype(BF16), tm=512)
    return out.reshape(bsz, seq, D_MODEL)
```

```python
import functools

import jax
import jax.numpy as jnp
import numpy as np
from jax import lax
from jax.experimental import pallas as pl
from jax.experimental.pallas import tpu as pltpu

D_MODEL = 1024
MLA_HEADS = 4
QK_NOPE = 128
QK_ROPE = 64
V_HEAD = 128
Q_LORA = 256
KV_LORA = 256
QK_HEAD = QK_NOPE + QK_ROPE
HG_HEADS = 4
HG_DK = 128
HG_DV = 128
HG_W = HG_HEADS * HG_DK
CHUNK = 64
D_FF = 2816
PLE_DIM = 256
ROPE_THETA = 10000.0
EPS = 1e-6

LANES = 128
QK_PAD = 2 * LANES
VMEM_LIMIT = 56 * 1024 * 1024

BF16 = jnp.bfloat16
F32 = jnp.float32


def _const_spec(shape):
    return pl.BlockSpec(shape, lambda *_: (0,) * len(shape), pipeline_mode=pl.Buffered(1))


def _sigmoid(x):
    return 0.5 + 0.5 * jnp.tanh(0.5 * x)


def _silu(x):
    h = 0.5 * x
    return h + h * jnp.tanh(h)


def _rms_scale(ss, n):
    return lax.rsqrt(ss * (1.0 / n) + EPS)


N_FREQ = QK_ROPE // 2
TOK_PER_ROW = LANES // N_FREQ


def _split3(x):
    p1 = x.astype(BF16)
    r1 = x - p1.astype(F32)
    p2 = r1.astype(BF16)
    p3 = (r1 - p2.astype(F32)).astype(BF16)
    return p1, p2, p3


def _rope_kernel(pos_ref, freq_ref, selc_ref, sels_ref, cos_ref, sin_ref):
    rows = pos_ref.shape[0]
    pos = pos_ref[...].astype(F32)
    lane = lax.broadcasted_iota(jnp.int32, (rows, LANES), 1)
    pos_d = jnp.broadcast_to(pos[:, TOK_PER_ROW - 1:TOK_PER_ROW], (rows, LANES))
    for i in range(TOK_PER_ROW - 2, -1, -1):
        pos_d = jnp.where(lane < (i + 1) * N_FREQ, jnp.broadcast_to(pos[:, i:i + 1], (rows, LANES)), pos_d)
    ang = pos_d * freq_ref[...]
    for trig, sel_ref, out_ref in ((jnp.cos(ang), selc_ref, cos_ref), (jnp.sin(ang), sels_ref, sin_ref)):
        pieces = jnp.concatenate(_split3(trig), axis=-1)
        for i in range(TOK_PER_ROW):
            out_ref[pl.ds(i, rows, stride=TOK_PER_ROW), :] = jnp.dot(
                pieces, sel_ref[i], preferred_element_type=F32)


def _rope_select_matrices():
    selc = np.zeros((TOK_PER_ROW, 3 * LANES, LANES), np.float32)
    sels = np.zeros_like(selc)
    j = np.arange(N_FREQ)
    for i in range(TOK_PER_ROW):
        for piece in range(3):
            for m in range(LANES // N_FREQ):
                selc[i, piece * LANES + N_FREQ * i + j, N_FREQ * m + j] = 1.0
                sels[i, piece * LANES + N_FREQ * i + j, N_FREQ * m + j] = 1.0 if m % 2 else -1.0
    return jnp.asarray(selc, BF16), jnp.asarray(sels, BF16)


def _rope_tables(pos4, freq_row):
    rows = pos4.shape[0]
    blk = 1024
    selc, sels = _rope_select_matrices()
    out = pl.BlockSpec((blk * TOK_PER_ROW, LANES), lambda i: (i, 0))
    return pl.pallas_call(
        _rope_kernel,
        grid=(rows // blk,),
        in_specs=[pl.BlockSpec((blk, TOK_PER_ROW), lambda i: (i, 0)), _const_spec(freq_row.shape),
                  _const_spec(selc.shape), _const_spec(sels.shape)],
        out_specs=[out, out],
        out_shape=[jax.ShapeDtypeStruct((rows * TOK_PER_ROW, LANES), F32)] * 2,
        compiler_params=pltpu.CompilerParams(dimension_semantics=("arbitrary",)),
        name="rope_tables",
    )(pos4, freq_row, selc, sels)


def _in_proj_kernel(x_ref, cos_ref, sin_ref, gmix_ref, w1_ref, gqa_ref, gkva_ref, wq2_ref, wkv2_ref,
                    gq_ref, gk_ref, lb_ref,
                    q_ref, k_ref, v_ref, hq_ref, hv_ref, kf_ref, kb_ref, lhf_ref, llf_ref, lhb_ref, llb_ref,
                    gs_ref):
    x = x_ref[...]
    h = x * _rms_scale(jnp.sum(x * x, axis=-1, keepdims=True), D_MODEL) * gmix_ref[...]
    z = jnp.dot(h.astype(BF16), w1_ref[...], preferred_element_type=F32)

    c_q = z[:, 0:Q_LORA]
    c_kv = z[:, Q_LORA:Q_LORA + KV_LORA]
    o = Q_LORA + KV_LORA
    hq = z[:, o:o + HG_W]
    hff = z[:, o + HG_W:o + 2 * HG_W]
    hfb = z[:, o + 2 * HG_W:o + 3 * HG_W]
    hi = z[:, o + 3 * HG_W:o + 4 * HG_W]
    hg = z[:, o + 4 * HG_W:o + 5 * HG_W]
    o += 5 * HG_W
    k_r = z[:, o:o + LANES]
    k_rs = z[:, o + LANES:o + 2 * LANES]

    cos = cos_ref[...]
    sin = sin_ref[...]

    cqn = c_q * _rms_scale(jnp.sum(c_q * c_q, axis=-1, keepdims=True), Q_LORA) * gqa_ref[...]
    q_all = jnp.dot(cqn.astype(BF16), wq2_ref[...], preferred_element_type=F32)
    gq_n, gq_r, gq_rs = gq_ref[0:1, :], gq_ref[1:2, :], gq_ref[2:3, :]
    scale = QK_HEAD ** -0.5 * np.log2(np.e)
    nh = MLA_HEADS
    for hd in range(nh):
        qn = q_all[:, hd * LANES:(hd + 1) * LANES]
        qr = q_all[:, (nh + hd) * LANES:(nh + hd + 1) * LANES]
        qrs = q_all[:, (2 * nh + hd) * LANES:(2 * nh + hd + 1) * LANES]
        ss = jnp.sum(qn * qn + qr * qr, axis=-1, keepdims=True)
        r = _rms_scale(ss, QK_HEAD) * scale
        q_ref[:, hd * QK_PAD:hd * QK_PAD + LANES] = (qn * gq_n * r).astype(BF16)
        q_ref[:, hd * QK_PAD + LANES:(hd + 1) * QK_PAD] = (
            (qr * gq_r * cos + qrs * gq_rs * sin) * r).astype(BF16)

    ckvn = c_kv * _rms_scale(jnp.sum(c_kv * c_kv, axis=-1, keepdims=True), KV_LORA) * gkva_ref[...]
    kv_all = jnp.dot(ckvn.astype(BF16), wkv2_ref[...], preferred_element_type=F32)
    gk_n, gk_r, gk_rs = gk_ref[0:1, :], gk_ref[1:2, :], gk_ref[2:3, :]
    k_rot = k_r * gk_r * cos + k_rs * gk_rs * sin
    ss_r = jnp.sum(k_r * k_r, axis=-1, keepdims=True)
    for hd in range(nh):
        kn = kv_all[:, hd * LANES:(hd + 1) * LANES]
        ss = jnp.sum(kn * kn, axis=-1, keepdims=True) + ss_r
        r = _rms_scale(ss, QK_HEAD)
        k_ref[:, hd * QK_PAD:hd * QK_PAD + LANES] = (kn * gk_n * r).astype(BF16)
        k_ref[:, hd * QK_PAD + LANES:(hd + 1) * QK_PAD] = (k_rot * r).astype(BF16)
    v_ref[...] = kv_all[:, nh * LANES:].astype(BF16)

    hq_ref[...] = _silu(hq)
    hv_ref[...] = hi.astype(BF16)
    gs_ref[...] = _silu(hg).astype(BF16)
    for f_pre, lb, k_out, hi_out, lo_out in ((hff, lb_ref[0:1, :], kf_ref, lhf_ref, llf_ref),
                                             (hfb, lb_ref[1:2, :], kb_ref, lhb_ref, llb_ref)):
        c = 0.5 * (1.0 - lb)
        ct = c * jnp.tanh(0.5 * f_pre)
        f = (0.5 * (1.0 + lb)) + ct
        k_out[...] = c - ct
        lf = jnp.log2(f)
        lf_hi = lf.astype(BF16)
        hi_out[...] = lf_hi
        lo_out[...] = (lf - lf_hi.astype(F32)).astype(BF16)


def _in_proj(x2, cos_t, sin_t, gmix, w1, gqa, gkva, wq2, wkv2, gq, gk, lb, tm):
    t = x2.shape[0]
    row = lambda w: pl.BlockSpec((tm, w), lambda i: (i, 0))
    outs = [(MLA_HEADS * QK_PAD, BF16), (MLA_HEADS * QK_PAD, BF16), (MLA_HEADS * V_HEAD, BF16),
            (HG_W, F32), (HG_W, BF16), (HG_W, F32), (HG_W, F32),
            (HG_W, BF16), (HG_W, BF16), (HG_W, BF16), (HG_W, BF16), (HG_W, BF16)]
    return pl.pallas_call(
        _in_proj_kernel,
        grid=(t // tm,),
        in_specs=[row(D_MODEL), row(LANES), row(LANES),
                  _const_spec(gmix.shape), _const_spec(w1.shape), _const_spec(gqa.shape),
                  _const_spec(gkva.shape), _const_spec(wq2.shape), _const_spec(wkv2.shape),
                  _const_spec(gq.shape), _const_spec(gk.shape), _const_spec(lb.shape)],
        out_specs=[row(w) for w, _ in outs],
        out_shape=[jax.ShapeDtypeStruct((t, w), dt) for w, dt in outs],
        compiler_params=pltpu.CompilerParams(dimension_semantics=("arbitrary",),
                                             vmem_limit_bytes=VMEM_LIMIT),
        name="in_proj",
    )(x2, cos_t, sin_t, gmix, w1, gqa, gkva, wq2, wkv2, gq, gk, lb)


def _attn_kernel(q_ref, k_ref, v_ref, o_ref, s0_ref, s1_ref, vt_ref, *, tq, tk, unroll):
    seq = k_ref.shape[0]
    nq, nk = seq // tq, seq // tk
    nt = (((1,), (1,)), ((), ()))
    sub = 8

    def fold(x, op):
        return op(x.reshape(tk // sub, sub, tq), axis=0)

    def q_rows(i):
        start = i * tq
        return pl.ds(start if isinstance(start, int) else pl.multiple_of(start, tq), tq)

    vt_ref[...] = v_ref[...].astype(F32).T.astype(BF16)

    def step(i_next, s_w, i_cur, s_r, m_cur):
        if i_next is not None:
            q = q_ref[q_rows(i_next), :]
            mrun = jnp.full((sub, tq), -jnp.inf, F32)
        if i_cur is not None:
            lrun = jnp.zeros((sub, tq), F32)
            acc = jnp.zeros((V_HEAD, tq), F32)
        for kc in range(nk):
            ks = slice(kc * tk, (kc + 1) * tk)
            if i_next is not None:
                s_c = lax.dot_general(k_ref[ks, :], q, nt, preferred_element_type=F32)
                s_w[ks, :] = s_c
                mrun = jnp.maximum(mrun, fold(s_c, jnp.max))
            if i_cur is not None:
                p_c = jnp.exp2(s_r[ks, :].reshape(tk // sub, sub, tq) - m_cur[None]).reshape(tk, tq)
                lrun = lrun + fold(p_c, jnp.sum)
                acc = acc + jnp.dot(vt_ref[:, ks], p_c.astype(BF16), preferred_element_type=F32)
        if i_cur is not None:
            l = jnp.sum(lrun, axis=0, keepdims=True)
            o_ref[q_rows(i_cur), :] = (acc / l).T.astype(o_ref.dtype)
        if i_next is None:
            return None
        return jnp.broadcast_to(jnp.max(mrun, axis=0, keepdims=True), (sub, tq))

    m0 = step(0, s0_ref, None, None, None)

    bufs = (s0_ref, s1_ref)

    def run_steps(first, count, m):
        for u in range(count):
            m = step(first + u + 1, bufs[(u + 1) % 2], first + u, bufs[u % 2], m)
        return m

    trips = (nq - 1) // unroll
    m0 = lax.fori_loop(0, trips, lambda j, m: run_steps(j * unroll, unroll, m), m0)
    m0 = run_steps(trips * unroll, nq - 1 - trips * unroll, m0)
    step(None, None, nq - 1, bufs[(nq - 1) % 2], m0)


def _attention(q, k, v, tq, tk, unroll):
    b, s, _ = q.shape
    assert s % tq == 0 and s % tk == 0 and tk % LANES == 0 and unroll % 2 == 0
    return pl.pallas_call(
        functools.partial(_attn_kernel, tq=tq, tk=tk, unroll=unroll),
        grid=(b, MLA_HEADS),
        in_specs=[pl.BlockSpec((None, s, QK_PAD), lambda bi, hi: (bi, 0, hi)),
                  pl.BlockSpec((None, s, QK_PAD), lambda bi, hi: (bi, 0, hi)),
                  pl.BlockSpec((None, s, V_HEAD), lambda bi, hi: (bi, 0, hi))],
        out_specs=pl.BlockSpec((None, s, V_HEAD), lambda bi, hi: (bi, 0, hi)),
        out_shape=jax.ShapeDtypeStruct((b, s, MLA_HEADS * V_HEAD), BF16),
        scratch_shapes=[pltpu.VMEM((s, tq), F32), pltpu.VMEM((s, tq), F32),
                        pltpu.VMEM((V_HEAD, s), BF16)],
        compiler_params=pltpu.CompilerParams(dimension_semantics=("arbitrary", "arbitrary"),
                                             vmem_limit_bytes=VMEM_LIMIT),
        name="attention",
    )(q, k, v)


def _hgrn_kernel(qf_ref, vf_ref, kf_ref, lhf_ref, llf_ref, qb_ref, vb_ref, kb_ref, lhb_ref, llb_ref,
                 tri_ref, of_ref, ob_ref, st_ref, *, n_chunks):
    @pl.when(pl.program_id(1) == 0)
    def _():
        st_ref[...] = jnp.zeros_like(st_ref)

    t_idx = lax.broadcasted_iota(jnp.int32, (CHUNK, CHUNK), 0)
    s_idx = lax.broadcasted_iota(jnp.int32, (CHUNK, CHUNK), 1)
    streams = (
        (qf_ref, vf_ref, kf_ref, (lhf_ref, llf_ref), of_ref, t_idx >= s_idx, CHUNK // 2 - 1, CHUNK - 1),
        (qb_ref, vb_ref, kb_ref, (lhb_ref, llb_ref), ob_ref, t_idx <= s_idx, CHUNK // 2, 0),
    )

    nt = (((1,), (1,)), ((), ()))
    tn = (((0,), (0,)), ((), ()))
    heads = [slice(hd * HG_DK, (hd + 1) * HG_DK) for hd in range(HG_HEADS)]

    def chunk_of(d, i):
        return i if d == 0 else n_chunks - 1 - i

    cum, gated, intra = {}, {}, {}
    state = {(d, hd): st_ref[d, hd] for d in range(2) for hd in range(HG_HEADS)}

    def stage_cumsum(d, i):
        lh_ref, ll_ref = streams[d][3]
        rows = slice(chunk_of(d, i) * CHUNK, (chunk_of(d, i) + 1) * CHUNK)
        hi_lo = jnp.concatenate([lh_ref[rows, :], ll_ref[rows, :]], axis=0)
        cum[d, i] = jnp.dot(tri_ref[d], hi_lo, preferred_element_type=F32)

    def stage_gates(d, i):
        q_ref, _, k_ref, _, _, _, mid, last = streams[d]
        rows = slice(chunk_of(d, i) * CHUNK, (chunk_of(d, i) + 1) * CHUNK)
        b = cum.pop((d, i))
        b_mid = b[mid:mid + 1, :]
        b_last = b[last:last + 1, :]
        x = b - b_mid
        q_in = q_ref[rows, :] * jnp.exp2(x)
        k_in = k_ref[rows, :] * jnp.exp2(-x)
        q_st = (q_in * jnp.exp2(b_mid)).astype(BF16)
        k_st = (k_in * jnp.exp2(b_last - b_mid)).astype(BF16)
        gated[d, i] = (q_in.astype(BF16), k_in.astype(BF16), q_st, k_st, jnp.exp2(b_last))

    def stage_intra(d, i):
        v_ref, mask = streams[d][1], streams[d][5]
        rows = slice(chunk_of(d, i) * CHUNK, (chunk_of(d, i) + 1) * CHUNK)
        q_in, k_in, q_st, k_st, decay = gated.pop((d, i))
        v = v_ref[rows, :]
        a, kv_t = [], []
        for sl in heads:
            a_h = lax.dot_general(q_in[:, sl], k_in[:, sl], nt, preferred_element_type=F32)
            a.append(jnp.where(mask, a_h, 0.0).astype(BF16))
            kv_t.append(lax.dot_general(v[:, sl], k_st[:, sl], tn, preferred_element_type=F32))
        intra[d, i] = (rows, v, q_st, a, kv_t, decay)

    def stage_state(d, i):
        o_ref = streams[d][4]
        rows, v, q_st, a, kv_t, decay = intra.pop((d, i))
        for hd, sl in enumerate(heads):
            st = state[d, hd]
            o = jnp.dot(a[hd], v[:, sl], preferred_element_type=F32)
            o += lax.dot_general(q_st[:, sl], st.astype(BF16), nt, preferred_element_type=F32)
            o_ref[rows, sl] = o
            state[d, hd] = st * decay[:, sl] + kv_t[hd]

    stages = (stage_cumsum, stage_gates, stage_intra, stage_state)
    for t in range(n_chunks + len(stages) - 1):
        for lag, stage in enumerate(stages):
            if 0 <= t - lag < n_chunks:
                for d in range(2):
                    stage(d, t - lag)
    for (d, hd), st in state.items():
        st_ref[d, hd] = st


def _hgrn(hq, hv, kf, kb, lhf, llf, lhb, llb, tri, blk):
    b, s, _ = hq.shape
    nb = s // blk
    fwd = pl.BlockSpec((None, blk, HG_W), lambda bi, j: (bi, j, 0))
    bwd = pl.BlockSpec((None, blk, HG_W), lambda bi, j: (bi, nb - 1 - j, 0))
    return pl.pallas_call(
        functools.partial(_hgrn_kernel, n_chunks=blk // CHUNK),
        grid=(b, nb),
        in_specs=[fwd] * 5 + [bwd] * 5 + [_const_spec(tri.shape)],
        out_specs=[fwd, bwd],
        out_shape=[jax.ShapeDtypeStruct((b, s, HG_W), F32)] * 2,
        scratch_shapes=[pltpu.VMEM((2, HG_HEADS, HG_DV, HG_DK), F32)],
        compiler_params=pltpu.CompilerParams(dimension_semantics=("arbitrary", "arbitrary"),
                                             vmem_limit_bytes=VMEM_LIMIT),
        name="hgrn2",
    )(hq, hv, kf, lhf, llf, hq, hv, kb, lhb, llb, tri)


def _out_kernel(x_ref, a_ref, of_ref, ob_ref, gs_ref, p_ref, ghgo_ref, woa_ref, wor_ref, gffn_ref,
                wg_ref, wu_ref, wd_ref, gple_ref, wpg_ref, wpp_ref, out_ref, *, ff_chunk):
    o = of_ref[...] + ob_ref[...]
    parts = []
    for hd in range(HG_HEADS):
        oh = o[:, hd * HG_DV:(hd + 1) * HG_DV]
        parts.append(oh * _rms_scale(jnp.sum(oh * oh, axis=-1, keepdims=True), HG_DV))
    r = jnp.concatenate(parts, axis=-1) * ghgo_ref[...] * gs_ref[...].astype(F32)
    x = (x_ref[...]
         + jnp.dot(a_ref[...], woa_ref[...], preferred_element_type=F32)
         + jnp.dot(r.astype(BF16), wor_ref[...], preferred_element_type=F32))

    h = (x * _rms_scale(jnp.sum(x * x, axis=-1, keepdims=True), D_MODEL) * gffn_ref[...]).astype(BF16)
    ffn = None
    for c in range(D_FF // ff_chunk):
        cs = slice(c * ff_chunk, (c + 1) * ff_chunk)
        gate = jnp.dot(h, wg_ref[:, cs], preferred_element_type=F32)
        up = jnp.dot(h, wu_ref[:, cs], preferred_element_type=F32)
        act = (_silu(gate) * up).astype(BF16)
        down = jnp.dot(act, wd_ref[cs, :], preferred_element_type=F32)
        ffn = down if ffn is None else ffn + down
    x = x + ffn

    h = (x * _rms_scale(jnp.sum(x * x, axis=-1, keepdims=True), D_MODEL) * gple_ref[...]).astype(BF16)
    gate = _sigmoid(jnp.dot(h, wpg_ref[...], preferred_element_type=F32))
    proj = jnp.dot(p_ref[...].astype(BF16), wpp_ref[...], preferred_element_type=F32)
    out_ref[...] = x + gate * proj


def _out_ffn_ple(x2, a, o_f, o_b, gs, p2, ghgo, woa, wor, gffn, wg, wu, wd, gple, wpg, wpp, tm):
    t = x2.shape[0]
    row = lambda w: pl.BlockSpec((tm, w), lambda i: (i, 0))
    consts = (ghgo, woa, wor, gffn, wg, wu, wd, gple, wpg, wpp)
    return pl.pallas_call(
        functools.partial(_out_kernel, ff_chunk=256),
        grid=(t // tm,),
        in_specs=[row(D_MODEL), row(HG_W), row(HG_W), row(HG_W), row(HG_W), row(PLE_DIM)]
                 + [_const_spec(c.shape) for c in consts],
        out_specs=row(D_MODEL),
        out_shape=jax.ShapeDtypeStruct((t, D_MODEL), F32),
        compiler_params=pltpu.CompilerParams(dimension_semantics=("arbitrary",),
                                             vmem_limit_bytes=VMEM_LIMIT),
        name="out_ffn_ple",
    )(x2, a, o_f, o_b, gs, p2, *consts)


def _swap_halves(w):
    half = w.shape[-1] // 2
    return jnp.concatenate([w[..., half:], w[..., :half]], axis=-1)


def _pad_lanes(w):
    return jnp.pad(w, [(0, 0)] * (w.ndim - 1) + [(0, LANES - w.shape[-1])])


def kernel(x, p, positions, g_mix, w_in, g_qa, g_kva, w_qb, w_kvb, g_qn, g_kn, lb_param, g_hgo,
           w_o, g_ffn, w_gate, w_up, w_down, g_ple, w_ple_gate, w_ple_proj):
    bsz, seq, _ = x.shape
    t = bsz * seq
    layer = 0

    inv_freq = (ROPE_THETA ** (-np.arange(0, QK_ROPE, 2, dtype=np.float64) / QK_ROPE)).astype(np.float32)
    freq_row = jnp.asarray(np.tile(inv_freq, TOK_PER_ROW)[None, :])
    cos_t, sin_t = _rope_tables(positions.reshape(t // TOK_PER_ROW, TOK_PER_ROW), freq_row)

    wi = w_in[layer]
    c_q, c_kv, k_r, h5 = (wi[:, :Q_LORA], wi[:, Q_LORA:Q_LORA + KV_LORA],
                          wi[:, Q_LORA + KV_LORA:Q_LORA + KV_LORA + QK_ROPE],
                          wi[:, Q_LORA + KV_LORA + QK_ROPE:])
    w1 = jnp.concatenate([c_q, c_kv, h5, _pad_lanes(k_r), _pad_lanes(_swap_halves(k_r))], axis=1).astype(BF16)

    wq = w_qb[layer].reshape(Q_LORA, MLA_HEADS, QK_HEAD)
    wq_n = wq[:, :, :QK_NOPE].reshape(Q_LORA, -1)
    wq_r = _pad_lanes(wq[:, :, QK_NOPE:]).reshape(Q_LORA, -1)
    wq_rs = _pad_lanes(_swap_halves(wq[:, :, QK_NOPE:])).reshape(Q_LORA, -1)
    wq2 = jnp.concatenate([wq_n, wq_r, wq_rs], axis=1).astype(BF16)

    wkv = w_kvb[layer].reshape(KV_LORA, MLA_HEADS, QK_NOPE + V_HEAD)
    wkv2 = jnp.concatenate([wkv[:, :, :QK_NOPE].reshape(KV_LORA, -1),
                            wkv[:, :, QK_NOPE:].reshape(KV_LORA, -1)], axis=1).astype(BF16)

    def qk_gain(g):
        g_r = g[QK_NOPE:]
        return jnp.stack([g[:QK_NOPE], _pad_lanes(g_r), _pad_lanes(_swap_halves(g_r))]).astype(F32)

    lower = jax.nn.softmax(lb_param.astype(F32), axis=0)[layer]
    row = lambda g: g.reshape(1, -1).astype(F32)

    x2 = x.reshape(t, D_MODEL)
    q, k, v, hq, hv, kf, kb, lhf, llf, lhb, llb, gs = _in_proj(
        x2, cos_t, sin_t, row(g_mix[layer]), w1, row(g_qa[layer]), row(g_kva[layer]), wq2, wkv2,
        qk_gain(g_qn[layer]), qk_gain(g_kn[layer]), lower, tm=512)

    a = _attention(q.reshape(bsz, seq, -1), k.reshape(bsz, seq, -1), v.reshape(bsz, seq, -1), tq=256, tk=1024,
                   unroll=4)

    tt = np.arange(CHUNK)
    tri = np.stack([tt[:, None] >= tt[None, :], tt[:, None] <= tt[None, :]])
    tri = jnp.asarray(np.concatenate([tri, tri], axis=-1), dtype=BF16)
    r3 = lambda z: z.reshape(bsz, seq, HG_W)
    o_f, o_b = _hgrn(r3(hq), r3(hv), r3(kf), r3(kb), r3(lhf), r3(llf), r3(lhb), r3(llb), tri, blk=512)

    wo = w_o[layer].astype(BF16)
    out = _out_ffn_ple(
        x2, a.reshape(t, -1), o_f.reshape(t, HG_W), o_b.reshape(t, HG_W), gs, p[layer].reshape(t, PLE_DIM),
        row(g_hgo[layer]), wo[:MLA_HEADS * V_HEAD], wo[MLA_HEADS * V_HEAD:], row(g_ffn[layer]),
        w_gate[layer].astype(BF16), w_up[layer].astype(BF16), w_down[layer].astype(BF16),
        row(g_ple[layer]), w_ple_gate[layer].astype(BF16), w_ple_proj[layer].astype(BF16), tm=512)
    return out.reshape(bsz, seq, D_MODEL)
```

```python
import functools

import jax
import jax.numpy as jnp
import numpy as np
from jax import lax
from jax.experimental import pallas as pl
from jax.experimental.pallas import tpu as pltpu

D_MODEL = 1024
MLA_HEADS = 4
QK_NOPE = 128
QK_ROPE = 64
V_HEAD = 128
Q_LORA = 256
KV_LORA = 256
QK_HEAD = QK_NOPE + QK_ROPE
HG_HEADS = 4
HG_DK = 128
HG_DV = 128
HG_W = HG_HEADS * HG_DK
CHUNK = 64
D_FF = 2816
PLE_DIM = 256
ROPE_THETA = 10000.0
EPS = 1e-6

LANES = 128
QK_PAD = 2 * LANES
VMEM_LIMIT = 56 * 1024 * 1024

BF16 = jnp.bfloat16
F32 = jnp.float32


def _const_spec(shape):
    return pl.BlockSpec(shape, lambda *_: (0,) * len(shape), pipeline_mode=pl.Buffered(1))


def _sigmoid(x):
    return 0.5 + 0.5 * jnp.tanh(0.5 * x)


def _silu(x):
    h = 0.5 * x
    return h + h * jnp.tanh(h)


def _rms_scale(ss, n):
    return lax.rsqrt(ss * (1.0 / n) + EPS)


N_FREQ = QK_ROPE // 2
TOK_PER_ROW = LANES // N_FREQ


def _split3(x):
    p1 = x.astype(BF16)
    r1 = x - p1.astype(F32)
    p2 = r1.astype(BF16)
    p3 = (r1 - p2.astype(F32)).astype(BF16)
    return p1, p2, p3


def _rope_kernel(pos_ref, freq_ref, selc_ref, sels_ref, cos_ref, sin_ref):
    rows = pos_ref.shape[0]
    pos = pos_ref[...].astype(F32)
    lane = lax.broadcasted_iota(jnp.int32, (rows, LANES), 1)
    pos_d = jnp.broadcast_to(pos[:, TOK_PER_ROW - 1:TOK_PER_ROW], (rows, LANES))
    for i in range(TOK_PER_ROW - 2, -1, -1):
        pos_d = jnp.where(lane < (i + 1) * N_FREQ, jnp.broadcast_to(pos[:, i:i + 1], (rows, LANES)), pos_d)
    ang = pos_d * freq_ref[...]
    for trig, sel_ref, out_ref in ((jnp.cos(ang), selc_ref, cos_ref), (jnp.sin(ang), sels_ref, sin_ref)):
        pieces = jnp.concatenate(_split3(trig), axis=-1)
        for i in range(TOK_PER_ROW):
            out_ref[pl.ds(i, rows, stride=TOK_PER_ROW), :] = jnp.dot(
                pieces, sel_ref[i], preferred_element_type=F32)


def _rope_select_matrices():
    selc = np.zeros((TOK_PER_ROW, 3 * LANES, LANES), np.float32)
    sels = np.zeros_like(selc)
    j = np.arange(N_FREQ)
    for i in range(TOK_PER_ROW):
        for piece in range(3):
            for m in range(LANES // N_FREQ):
                selc[i, piece * LANES + N_FREQ * i + j, N_FREQ * m + j] = 1.0
                sels[i, piece * LANES + N_FREQ * i + j, N_FREQ * m + j] = 1.0 if m % 2 else -1.0
    return jnp.asarray(selc, BF16), jnp.asarray(sels, BF16)


def _rope_tables(pos4, freq_row):
    rows = pos4.shape[0]
    blk = 1024
    selc, sels = _rope_select_matrices()
    out = pl.BlockSpec((blk * TOK_PER_ROW, LANES), lambda i: (i, 0))
    return pl.pallas_call(
        _rope_kernel,
        grid=(rows // blk,),
        in_specs=[pl.BlockSpec((blk, TOK_PER_ROW), lambda i: (i, 0)), _const_spec(freq_row.shape),
                  _const_spec(selc.shape), _const_spec(sels.shape)],
        out_specs=[out, out],
        out_shape=[jax.ShapeDtypeStruct((rows * TOK_PER_ROW, LANES), F32)] * 2,
        compiler_params=pltpu.CompilerParams(dimension_semantics=("arbitrary",)),
        name="rope_tables",
    )(pos4, freq_row, selc, sels)


def _in_proj_kernel(x_ref, cos_ref, sin_ref, gmix_ref, w1_ref, gqa_ref, gkva_ref, wq2_ref, wkv2_ref,
                    gq_ref, gk_ref,
                    q_ref, k_ref, v_ref, hq_ref, hv_ref, hff_ref, hfb_ref, hg_ref):
    x = x_ref[...]
    h = x * _rms_scale(jnp.sum(x * x, axis=-1, keepdims=True), D_MODEL) * gmix_ref[...]
    z = jnp.dot(h.astype(BF16), w1_ref[...], preferred_element_type=F32)

    c_q = z[:, 0:Q_LORA]
    c_kv = z[:, Q_LORA:Q_LORA + KV_LORA]
    o = Q_LORA + KV_LORA
    hq = z[:, o:o + HG_W]
    hff = z[:, o + HG_W:o + 2 * HG_W]
    hfb = z[:, o + 2 * HG_W:o + 3 * HG_W]
    hi = z[:, o + 3 * HG_W:o + 4 * HG_W]
    hg = z[:, o + 4 * HG_W:o + 5 * HG_W]
    o += 5 * HG_W
    k_r = z[:, o:o + LANES]
    k_rs = z[:, o + LANES:o + 2 * LANES]

    cos = cos_ref[...]
    sin = sin_ref[...]

    cqn = c_q * _rms_scale(jnp.sum(c_q * c_q, axis=-1, keepdims=True), Q_LORA) * gqa_ref[...]
    q_all = jnp.dot(cqn.astype(BF16), wq2_ref[...], preferred_element_type=F32)
    gq_n, gq_r, gq_rs = gq_ref[0:1, :], gq_ref[1:2, :], gq_ref[2:3, :]
    scale = QK_HEAD ** -0.5 * np.log2(np.e)
    nh = MLA_HEADS
    for hd in range(nh):
        qn = q_all[:, hd * LANES:(hd + 1) * LANES]
        qr = q_all[:, (nh + hd) * LANES:(nh + hd + 1) * LANES]
        qrs = q_all[:, (2 * nh + hd) * LANES:(2 * nh + hd + 1) * LANES]
        ss = jnp.sum(qn * qn + qr * qr, axis=-1, keepdims=True)
        r = _rms_scale(ss, QK_HEAD) * scale
        q_ref[:, hd * QK_PAD:hd * QK_PAD + LANES] = (qn * gq_n * r).astype(BF16)
        q_ref[:, hd * QK_PAD + LANES:(hd + 1) * QK_PAD] = (
            (qr * gq_r * cos + qrs * gq_rs * sin) * r).astype(BF16)

    ckvn = c_kv * _rms_scale(jnp.sum(c_kv * c_kv, axis=-1, keepdims=True), KV_LORA) * gkva_ref[...]
    kv_all = jnp.dot(ckvn.astype(BF16), wkv2_ref[...], preferred_element_type=F32)
    gk_n, gk_r, gk_rs = gk_ref[0:1, :], gk_ref[1:2, :], gk_ref[2:3, :]
    k_rot = k_r * gk_r * cos + k_rs * gk_rs * sin
    ss_r = jnp.sum(k_r * k_r, axis=-1, keepdims=True)
    for hd in range(nh):
        kn = kv_all[:, hd * LANES:(hd + 1) * LANES]
        ss = jnp.sum(kn * kn, axis=-1, keepdims=True) + ss_r
        r = _rms_scale(ss, QK_HEAD)
        k_ref[:, hd * QK_PAD:hd * QK_PAD + LANES] = (kn * gk_n * r).astype(BF16)
        k_ref[:, hd * QK_PAD + LANES:(hd + 1) * QK_PAD] = (k_rot * r).astype(BF16)
    v_ref[...] = kv_all[:, nh * LANES:].astype(BF16)

    hq_ref[...] = hq
    hv_ref[...] = hi.astype(BF16)
    hff_ref[...] = hff
    hfb_ref[...] = hfb
    hg_ref[...] = hg.astype(BF16)


def _in_proj(x2, cos_t, sin_t, gmix, w1, gqa, gkva, wq2, wkv2, gq, gk, tm):
    t = x2.shape[0]
    row = lambda w: pl.BlockSpec((tm, w), lambda i: (i, 0))
    outs = [(MLA_HEADS * QK_PAD, BF16), (MLA_HEADS * QK_PAD, BF16), (MLA_HEADS * V_HEAD, BF16),
            (HG_W, F32), (HG_W, BF16), (HG_W, F32), (HG_W, F32), (HG_W, BF16)]
    return pl.pallas_call(
        _in_proj_kernel,
        grid=(t // tm,),
        in_specs=[row(D_MODEL), row(LANES), row(LANES),
                  _const_spec(gmix.shape), _const_spec(w1.shape), _const_spec(gqa.shape),
                  _const_spec(gkva.shape), _const_spec(wq2.shape), _const_spec(wkv2.shape),
                  _const_spec(gq.shape), _const_spec(gk.shape)],
        out_specs=[row(w) for w, _ in outs],
        out_shape=[jax.ShapeDtypeStruct((t, w), dt) for w, dt in outs],
        compiler_params=pltpu.CompilerParams(dimension_semantics=("arbitrary",),
                                             vmem_limit_bytes=VMEM_LIMIT),
        name="in_proj",
    )(x2, cos_t, sin_t, gmix, w1, gqa, gkva, wq2, wkv2, gq, gk)


def _attn_kernel(q_ref, k_ref, v_ref, o_ref, s0_ref, s1_ref, vt_ref, *, tq, tk, unroll):
    seq = k_ref.shape[0]
    nq, nk = seq // tq, seq // tk
    nt = (((1,), (1,)), ((), ()))
    sub = 8

    def fold(x, op):
        return op(x.reshape(tk // sub, sub, tq), axis=0)

    def q_rows(i):
        start = i * tq
        return pl.ds(start if isinstance(start, int) else pl.multiple_of(start, tq), tq)

    vt_ref[...] = v_ref[...].astype(F32).T.astype(BF16)

    def step(i_next, s_w, i_cur, s_r, m_cur):
        if i_next is not None:
            q = q_ref[q_rows(i_next), :]
            mrun = jnp.full((sub, tq), -jnp.inf, F32)
        if i_cur is not None:
            lrun = jnp.zeros((sub, tq), F32)
            acc = jnp.zeros((V_HEAD, tq), F32)
        for kc in range(nk):
            ks = slice(kc * tk, (kc + 1) * tk)
            if i_next is not None:
                s_c = lax.dot_general(k_ref[ks, :], q, nt, preferred_element_type=F32)
                s_w[ks, :] = s_c
                mrun = jnp.maximum(mrun, fold(s_c, jnp.max))
            if i_cur is not None:
                p_c = jnp.exp2(s_r[ks, :].reshape(tk // sub, sub, tq) - m_cur[None]).reshape(tk, tq)
                lrun = lrun + fold(p_c, jnp.sum)
                acc = acc + jnp.dot(vt_ref[:, ks], p_c.astype(BF16), preferred_element_type=F32)
        if i_cur is not None:
            l = jnp.sum(lrun, axis=0, keepdims=True)
            o_ref[q_rows(i_cur), :] = (acc / l).T.astype(o_ref.dtype)
        if i_next is None:
            return None
        return jnp.broadcast_to(jnp.max(mrun, axis=0, keepdims=True), (sub, tq))

    m0 = step(0, s0_ref, None, None, None)

    bufs = (s0_ref, s1_ref)

    def run_steps(first, count, m):
        for u in range(count):
            m = step(first + u + 1, bufs[(u + 1) % 2], first + u, bufs[u % 2], m)
        return m

    trips = (nq - 1) // unroll
    m0 = lax.fori_loop(0, trips, lambda j, m: run_steps(j * unroll, unroll, m), m0)
    m0 = run_steps(trips * unroll, nq - 1 - trips * unroll, m0)
    step(None, None, nq - 1, bufs[(nq - 1) % 2], m0)


def _attention(q, k, v, tq, tk, unroll):
    b, s, _ = q.shape
    assert s % tq == 0 and s % tk == 0 and tk % LANES == 0 and unroll % 2 == 0
    return pl.pallas_call(
        functools.partial(_attn_kernel, tq=tq, tk=tk, unroll=unroll),
        grid=(b, MLA_HEADS),
        in_specs=[pl.BlockSpec((None, s, QK_PAD), lambda bi, hi: (bi, 0, hi)),
                  pl.BlockSpec((None, s, QK_PAD), lambda bi, hi: (bi, 0, hi)),
                  pl.BlockSpec((None, s, V_HEAD), lambda bi, hi: (bi, 0, hi))],
        out_specs=pl.BlockSpec((None, s, V_HEAD), lambda bi, hi: (bi, 0, hi)),
        out_shape=jax.ShapeDtypeStruct((b, s, MLA_HEADS * V_HEAD), BF16),
        scratch_shapes=[pltpu.VMEM((s, tq), F32), pltpu.VMEM((s, tq), F32),
                        pltpu.VMEM((V_HEAD, s), BF16)],
        compiler_params=pltpu.CompilerParams(dimension_semantics=("arbitrary", "arbitrary"),
                                             vmem_limit_bytes=VMEM_LIMIT),
        name="attention",
    )(q, k, v)


def _hgrn_kernel(qf_ref, vf_ref, ff_ref, qb_ref, vb_ref, fb_ref, lb_ref, tri_ref,
                 of_ref, ob_ref, st_ref, *, n_chunks):
    @pl.when(pl.program_id(1) == 0)
    def _():
        st_ref[...] = jnp.zeros_like(st_ref)

    t_idx = lax.broadcasted_iota(jnp.int32, (CHUNK, CHUNK), 0)
    s_idx = lax.broadcasted_iota(jnp.int32, (CHUNK, CHUNK), 1)
    streams = (
        (qf_ref, vf_ref, ff_ref, None, of_ref, t_idx >= s_idx, CHUNK // 2 - 1, CHUNK - 1),
        (qb_ref, vb_ref, fb_ref, None, ob_ref, t_idx <= s_idx, CHUNK // 2, 0),
    )
    gate_half = [0.5 * (1.0 - lb_ref[d:d + 1, :]) for d in range(2)]
    gate_mid = [0.5 * (1.0 + lb_ref[d:d + 1, :]) for d in range(2)]

    nt = (((1,), (1,)), ((), ()))
    tn = (((0,), (0,)), ((), ()))
    heads = [slice(hd * HG_DK, (hd + 1) * HG_DK) for hd in range(HG_HEADS)]

    def chunk_of(d, i):
        return i if d == 0 else n_chunks - 1 - i

    cum, gated, intra = {}, {}, {}
    state = {(d, hd): st_ref[d, hd] for d in range(2) for hd in range(HG_HEADS)}

    def stage_cumsum(d, i):
        f_ref = streams[d][2]
        rows = slice(chunk_of(d, i) * CHUNK, (chunk_of(d, i) + 1) * CHUNK)
        ct = gate_half[d] * jnp.tanh(0.5 * f_ref[rows, :])
        k = gate_half[d] - ct
        lf = jnp.log2(gate_mid[d] + ct)
        lf_hi = lf.astype(BF16)
        lf_lo = (lf - lf_hi.astype(F32)).astype(BF16)
        hi_lo = jnp.concatenate([lf_hi, lf_lo], axis=0)
        cum[d, i] = (jnp.dot(tri_ref[d], hi_lo, preferred_element_type=F32), k)

    def stage_gates(d, i):
        q_ref, _, _, _, _, _, mid, last = streams[d]
        rows = slice(chunk_of(d, i) * CHUNK, (chunk_of(d, i) + 1) * CHUNK)
        b, k = cum.pop((d, i))
        b_mid = b[mid:mid + 1, :]
        b_last = b[last:last + 1, :]
        x = b - b_mid
        q_in = _silu(q_ref[rows, :]) * jnp.exp2(x)
        k_in = k * jnp.exp2(-x)
        q_st = (q_in * jnp.exp2(b_mid)).astype(BF16)
        k_st = (k_in * jnp.exp2(b_last - b_mid)).astype(BF16)
        gated[d, i] = (q_in.astype(BF16), k_in.astype(BF16), q_st, k_st, jnp.exp2(b_last))

    def stage_intra(d, i):
        v_ref, mask = streams[d][1], streams[d][5]
        rows = slice(chunk_of(d, i) * CHUNK, (chunk_of(d, i) + 1) * CHUNK)
        q_in, k_in, q_st, k_st, decay = gated.pop((d, i))
        v = v_ref[rows, :]
        a, kv_t = [], []
        for sl in heads:
            a_h = lax.dot_general(q_in[:, sl], k_in[:, sl], nt, preferred_element_type=F32)
            a.append(jnp.where(mask, a_h, 0.0).astype(BF16))
            kv_t.append(lax.dot_general(v[:, sl], k_st[:, sl], tn, preferred_element_type=F32))
        intra[d, i] = (rows, v, q_st, a, kv_t, decay)

    def stage_state(d, i):
        o_ref = streams[d][4]
        rows, v, q_st, a, kv_t, decay = intra.pop((d, i))
        for hd, sl in enumerate(heads):
            st = state[d, hd]
            o = jnp.dot(a[hd], v[:, sl], preferred_element_type=F32)
            o += lax.dot_general(q_st[:, sl], st.astype(BF16), nt, preferred_element_type=F32)
            o_ref[rows, sl] = o
            state[d, hd] = st * decay[:, sl] + kv_t[hd]

    stages = (stage_cumsum, stage_gates, stage_intra, stage_state)
    for t in range(n_chunks + len(stages) - 1):
        for lag, stage in enumerate(stages):
            if 0 <= t - lag < n_chunks:
                for d in range(2):
                    stage(d, t - lag)
    for (d, hd), st in state.items():
        st_ref[d, hd] = st


def _hgrn(hq, hv, hff, hfb, lb, tri, blk):
    b, s, _ = hq.shape
    nb = s // blk
    fwd = pl.BlockSpec((None, blk, HG_W), lambda bi, j: (bi, j, 0))
    bwd = pl.BlockSpec((None, blk, HG_W), lambda bi, j: (bi, nb - 1 - j, 0))
    return pl.pallas_call(
        functools.partial(_hgrn_kernel, n_chunks=blk // CHUNK),
        grid=(b, nb),
        in_specs=[fwd] * 3 + [bwd] * 3 + [_const_spec(lb.shape), _const_spec(tri.shape)],
        out_specs=[fwd, bwd],
        out_shape=[jax.ShapeDtypeStruct((b, s, HG_W), F32)] * 2,
        scratch_shapes=[pltpu.VMEM((2, HG_HEADS, HG_DV, HG_DK), F32)],
        compiler_params=pltpu.CompilerParams(dimension_semantics=("arbitrary", "arbitrary"),
                                             vmem_limit_bytes=VMEM_LIMIT),
        name="hgrn2",
    )(hq, hv, hff, hq, hv, hfb, lb, tri)


def _out_kernel(x_ref, a_ref, of_ref, ob_ref, hg_ref, p_ref, ghgo_ref, woa_ref, wor_ref, gffn_ref,
                wg_ref, wu_ref, wd_ref, gple_ref, wpg_ref, wpp_ref, out_ref, *, ff_chunk):
    o = of_ref[...] + ob_ref[...]
    parts = []
    for hd in range(HG_HEADS):
        oh = o[:, hd * HG_DV:(hd + 1) * HG_DV]
        parts.append(oh * _rms_scale(jnp.sum(oh * oh, axis=-1, keepdims=True), HG_DV))
    r = jnp.concatenate(parts, axis=-1) * ghgo_ref[...] * _silu(hg_ref[...].astype(F32))
    x = (x_ref[...]
         + jnp.dot(a_ref[...], woa_ref[...], preferred_element_type=F32)
         + jnp.dot(r.astype(BF16), wor_ref[...], preferred_element_type=F32))

    h = (x * _rms_scale(jnp.sum(x * x, axis=-1, keepdims=True), D_MODEL) * gffn_ref[...]).astype(BF16)
    ffn = None
    for c in range(D_FF // ff_chunk):
        cs = slice(c * ff_chunk, (c + 1) * ff_chunk)
        gate = jnp.dot(h, wg_ref[:, cs], preferred_element_type=F32)
        up = jnp.dot(h, wu_ref[:, cs], preferred_element_type=F32)
        act = (_silu(gate) * up).astype(BF16)
        down = jnp.dot(act, wd_ref[cs, :], preferred_element_type=F32)
        ffn = down if ffn is None else ffn + down
    x = x + ffn

    h = (x * _rms_scale(jnp.sum(x * x, axis=-1, keepdims=True), D_MODEL) * gple_ref[...]).astype(BF16)
    gate = _sigmoid(jnp.dot(h, wpg_ref[...], preferred_element_type=F32))
    proj = jnp.dot(p_ref[...].astype(BF16), wpp_ref[...], preferred_element_type=F32)
    out_ref[...] = x + gate * proj


def _out_ffn_ple(x2, a, o_f, o_b, hg, p2, ghgo, woa, wor, gffn, wg, wu, wd, gple, wpg, wpp, tm):
    t = x2.shape[0]
    row = lambda w: pl.BlockSpec((tm, w), lambda i: (i, 0))
    consts = (ghgo, woa, wor, gffn, wg, wu, wd, gple, wpg, wpp)
    return pl.pallas_call(
        functools.partial(_out_kernel, ff_chunk=256),
        grid=(t // tm,),
        in_specs=[row(D_MODEL), row(HG_W), row(HG_W), row(HG_W), row(HG_W), row(PLE_DIM)]
                 + [_const_spec(c.shape) for c in consts],
        out_specs=row(D_MODEL),
        out_shape=jax.ShapeDtypeStruct((t, D_MODEL), F32),
        compiler_params=pltpu.CompilerParams(dimension_semantics=("arbitrary",),
                                             vmem_limit_bytes=VMEM_LIMIT),
        name="out_ffn_ple",
    )(x2, a, o_f, o_b, hg, p2, *consts)


def _swap_halves(w):
    half = w.shape[-1] // 2
    return jnp.concatenate([w[..., half:], w[..., :half]], axis=-1)


def _pad_lanes(w):
    return jnp.pad(w, [(0, 0)] * (w.ndim - 1) + [(0, LANES - w.shape[-1])])


def kernel(x, p, positions, g_mix, w_in, g_qa, g_kva, w_qb, w_kvb, g_qn, g_kn, lb_param, g_hgo,
           w_o, g_ffn, w_gate, w_up, w_down, g_ple, w_ple_gate, w_ple_proj):
    bsz, seq, _ = x.shape
    t = bsz * seq
    layer = 0

    inv_freq = (ROPE_THETA ** (-np.arange(0, QK_ROPE, 2, dtype=np.float64) / QK_ROPE)).astype(np.float32)
    freq_row = jnp.asarray(np.tile(inv_freq, TOK_PER_ROW)[None, :])
    cos_t, sin_t = _rope_tables(positions.reshape(t // TOK_PER_ROW, TOK_PER_ROW), freq_row)

    wi = w_in[layer]
    c_q, c_kv, k_r, h5 = (wi[:, :Q_LORA], wi[:, Q_LORA:Q_LORA + KV_LORA],
                          wi[:, Q_LORA + KV_LORA:Q_LORA + KV_LORA + QK_ROPE],
                          wi[:, Q_LORA + KV_LORA + QK_ROPE:])
    w1 = jnp.concatenate([c_q, c_kv, h5, _pad_lanes(k_r), _pad_lanes(_swap_halves(k_r))], axis=1).astype(BF16)

    wq = w_qb[layer].reshape(Q_LORA, MLA_HEADS, QK_HEAD)
    wq_n = wq[:, :, :QK_NOPE].reshape(Q_LORA, -1)
    wq_r = _pad_lanes(wq[:, :, QK_NOPE:]).reshape(Q_LORA, -1)
    wq_rs = _pad_lanes(_swap_halves(wq[:, :, QK_NOPE:])).reshape(Q_LORA, -1)
    wq2 = jnp.concatenate([wq_n, wq_r, wq_rs], axis=1).astype(BF16)

    wkv = w_kvb[layer].reshape(KV_LORA, MLA_HEADS, QK_NOPE + V_HEAD)
    wkv2 = jnp.concatenate([wkv[:, :, :QK_NOPE].reshape(KV_LORA, -1),
                            wkv[:, :, QK_NOPE:].reshape(KV_LORA, -1)], axis=1).astype(BF16)

    def qk_gain(g):
        g_r = g[QK_NOPE:]
        return jnp.stack([g[:QK_NOPE], _pad_lanes(g_r), _pad_lanes(_swap_halves(g_r))]).astype(F32)

    lower = jax.nn.softmax(lb_param.astype(F32), axis=0)[layer]
    row = lambda g: g.reshape(1, -1).astype(F32)

    x2 = x.reshape(t, D_MODEL)
    q, k, v, hq, hv, hff, hfb, hg = _in_proj(
        x2, cos_t, sin_t, row(g_mix[layer]), w1, row(g_qa[layer]), row(g_kva[layer]), wq2, wkv2,
        qk_gain(g_qn[layer]), qk_gain(g_kn[layer]), tm=512)

    a = _attention(q.reshape(bsz, seq, -1), k.reshape(bsz, seq, -1), v.reshape(bsz, seq, -1), tq=256, tk=1024,
                   unroll=4)

    tt = np.arange(CHUNK)
    tri = np.stack([tt[:, None] >= tt[None, :], tt[:, None] <= tt[None, :]])
    tri = jnp.asarray(np.concatenate([tri, tri], axis=-1), dtype=BF16)
    r3 = lambda z: z.reshape(bsz, seq, HG_W)
    o_f, o_b = _hgrn(r3(hq), r3(hv), r3(hff), r3(hfb), lower, tri, blk=1024)

    wo = w_o[layer].astype(BF16)
    out = _out_ffn_ple(
        x2, a.reshape(t, -1), o_f.reshape(t, HG_W), o_b.reshape(t, HG_W), hg, p[layer].reshape(t, PLE_DIM),
        row(g_hgo[layer]), wo[:MLA_HEADS * V_HEAD], wo[MLA_HEADS * V_HEAD:], row(g_ffn[layer]),
        w_gate[layer].astype(BF16), w_up[layer].astype(BF16), w_down[layer].astype(BF16),
        row(g_ple[layer]), w_ple_gate[layer].astype(BF16), w_ple_proj[layer].astype(BF16), tm=512)
    return out.reshape(bsz, seq, D_MODEL)
```

```python
import functools

import jax
import jax.numpy as jnp
import numpy as np
from jax import lax
from jax.experimental import pallas as pl
from jax.experimental.pallas import tpu as pltpu

D_MODEL = 1024
MLA_HEADS = 4
QK_NOPE = 128
QK_ROPE = 64
V_HEAD = 128
Q_LORA = 256
KV_LORA = 256
QK_HEAD = QK_NOPE + QK_ROPE
HG_HEADS = 4
HG_DK = 128
HG_DV = 128
HG_W = HG_HEADS * HG_DK
CHUNK = 64
D_FF = 2816
PLE_DIM = 256
ROPE_THETA = 10000.0
EPS = 1e-6

LANES = 128
SUBLANES = 8
MXU_TILE = 256
QK_PAD = MXU_TILE
VMEM_LIMIT = 56 * 1024 * 1024

IN_PROJ_ROWS = 512
OUT_ROWS = 512
FF_CHUNK = MXU_TILE
ATTN_Q_BLOCK = MXU_TILE
ATTN_KEY_CHUNK = 1024
ATTN_UNROLL = 6
HGRN_BLOCK = 1024

BF16 = jnp.bfloat16
F32 = jnp.float32


def _const_spec(shape):
    return pl.BlockSpec(shape, lambda *_: (0,) * len(shape), pipeline_mode=pl.Buffered(1))


def _sigmoid(x):
    return 0.5 + 0.5 * jnp.tanh(0.5 * x)


def _silu(x):
    h = 0.5 * x
    return h + h * jnp.tanh(h)


def _rms_scale(ss, n):
    return lax.rsqrt(ss * (1.0 / n) + EPS)


N_FREQ = QK_ROPE // 2
TOK_PER_ROW = LANES // N_FREQ


def _split3(x):
    p1 = x.astype(BF16)
    r1 = x - p1.astype(F32)
    p2 = r1.astype(BF16)
    p3 = (r1 - p2.astype(F32)).astype(BF16)
    return p1, p2, p3


def _rope_kernel(pos_ref, freq_ref, selc_ref, sels_ref, cos_ref, sin_ref):
    rows = pos_ref.shape[0]
    pos = pos_ref[...].astype(F32)
    lane = lax.broadcasted_iota(jnp.int32, (rows, LANES), 1)
    pos_d = jnp.broadcast_to(pos[:, TOK_PER_ROW - 1:TOK_PER_ROW], (rows, LANES))
    for i in range(TOK_PER_ROW - 2, -1, -1):
        pos_d = jnp.where(lane < (i + 1) * N_FREQ, jnp.broadcast_to(pos[:, i:i + 1], (rows, LANES)), pos_d)
    ang = pos_d * freq_ref[...]
    for trig, sel_ref, out_ref in ((jnp.cos(ang), selc_ref, cos_ref), (jnp.sin(ang), sels_ref, sin_ref)):
        pieces = jnp.concatenate(_split3(trig), axis=-1)
        for i in range(TOK_PER_ROW):
            out_ref[pl.ds(i, rows, stride=TOK_PER_ROW), :] = jnp.dot(
                pieces, sel_ref[i], preferred_element_type=F32)


def _rope_select_matrices():
    selc = np.zeros((TOK_PER_ROW, 3 * LANES, LANES), np.float32)
    sels = np.zeros_like(selc)
    j = np.arange(N_FREQ)
    for i in range(TOK_PER_ROW):
        for piece in range(3):
            for m in range(LANES // N_FREQ):
                selc[i, piece * LANES + N_FREQ * i + j, N_FREQ * m + j] = 1.0
                sels[i, piece * LANES + N_FREQ * i + j, N_FREQ * m + j] = 1.0 if m % 2 else -1.0
    return jnp.asarray(selc, BF16), jnp.asarray(sels, BF16)


def _rope_tables(pos4, freq_row):
    rows = pos4.shape[0]
    blk = 1024
    selc, sels = _rope_select_matrices()
    out = pl.BlockSpec((blk * TOK_PER_ROW, LANES), lambda i: (i, 0))
    return pl.pallas_call(
        _rope_kernel,
        grid=(rows // blk,),
        in_specs=[pl.BlockSpec((blk, TOK_PER_ROW), lambda i: (i, 0)), _const_spec(freq_row.shape),
                  _const_spec(selc.shape), _const_spec(sels.shape)],
        out_specs=[out, out],
        out_shape=[jax.ShapeDtypeStruct((rows * TOK_PER_ROW, LANES), F32)] * 2,
        compiler_params=pltpu.CompilerParams(dimension_semantics=("arbitrary",)),
        name="rope_tables",
    )(pos4, freq_row, selc, sels)


def _in_proj_kernel(x_ref, cos_ref, sin_ref, gmix_ref, w1_ref, gqa_ref, gkva_ref, wq2_ref, wkv2_ref,
                    gq_ref, gk_ref,
                    q_ref, k_ref, v_ref, hq_ref, hv_ref, hff_ref, hfb_ref, hg_ref):
    x = x_ref[...]
    h = x * _rms_scale(jnp.sum(x * x, axis=-1, keepdims=True), D_MODEL) * gmix_ref[...]
    z = jnp.dot(h.astype(BF16), w1_ref[...], preferred_element_type=F32)

    c_q = z[:, 0:Q_LORA]
    c_kv = z[:, Q_LORA:Q_LORA + KV_LORA]
    o = Q_LORA + KV_LORA
    hq = z[:, o:o + HG_W]
    hff = z[:, o + HG_W:o + 2 * HG_W]
    hfb = z[:, o + 2 * HG_W:o + 3 * HG_W]
    hi = z[:, o + 3 * HG_W:o + 4 * HG_W]
    hg = z[:, o + 4 * HG_W:o + 5 * HG_W]
    o += 5 * HG_W
    k_r = z[:, o:o + LANES]
    k_rs = z[:, o + LANES:o + 2 * LANES]

    cos = cos_ref[...]
    sin = sin_ref[...]

    cqn = c_q * _rms_scale(jnp.sum(c_q * c_q, axis=-1, keepdims=True), Q_LORA) * gqa_ref[...]
    q_all = jnp.dot(cqn.astype(BF16), wq2_ref[...], preferred_element_type=F32)
    gq_n, gq_r, gq_rs = gq_ref[0:1, :], gq_ref[1:2, :], gq_ref[2:3, :]
    scale = QK_HEAD ** -0.5 * np.log2(np.e)
    nh = MLA_HEADS
    for hd in range(nh):
        qn = q_all[:, hd * LANES:(hd + 1) * LANES]
        qr = q_all[:, (nh + hd) * LANES:(nh + hd + 1) * LANES]
        qrs = q_all[:, (2 * nh + hd) * LANES:(2 * nh + hd + 1) * LANES]
        ss = jnp.sum(qn * qn + qr * qr, axis=-1, keepdims=True)
        r = _rms_scale(ss, QK_HEAD) * scale
        q_ref[:, hd * QK_PAD:hd * QK_PAD + LANES] = (qn * gq_n * r).astype(BF16)
        q_ref[:, hd * QK_PAD + LANES:(hd + 1) * QK_PAD] = (
            (qr * gq_r * cos + qrs * gq_rs * sin) * r).astype(BF16)

    ckvn = c_kv * _rms_scale(jnp.sum(c_kv * c_kv, axis=-1, keepdims=True), KV_LORA) * gkva_ref[...]
    kv_all = jnp.dot(ckvn.astype(BF16), wkv2_ref[...], preferred_element_type=F32)
    gk_n, gk_r, gk_rs = gk_ref[0:1, :], gk_ref[1:2, :], gk_ref[2:3, :]
    k_rot = k_r * gk_r * cos + k_rs * gk_rs * sin
    ss_r = jnp.sum(k_r * k_r, axis=-1, keepdims=True)
    for hd in range(nh):
        kn = kv_all[:, hd * LANES:(hd + 1) * LANES]
        ss = jnp.sum(kn * kn, axis=-1, keepdims=True) + ss_r
        r = _rms_scale(ss, QK_HEAD)
        k_ref[:, hd * QK_PAD:hd * QK_PAD + LANES] = (kn * gk_n * r).astype(BF16)
        k_ref[:, hd * QK_PAD + LANES:(hd + 1) * QK_PAD] = (k_rot * r).astype(BF16)
    v_ref[...] = kv_all[:, nh * LANES:].astype(BF16)

    hq_ref[...] = hq
    hv_ref[...] = hi.astype(BF16)
    hff_ref[...] = hff
    hfb_ref[...] = hfb
    hg_ref[...] = hg.astype(BF16)


def _in_proj(x2, cos_t, sin_t, gmix, w1, gqa, gkva, wq2, wkv2, gq, gk, tm):
    t = x2.shape[0]
    row = lambda w: pl.BlockSpec((tm, w), lambda i: (i, 0))
    outs = [(MLA_HEADS * QK_PAD, BF16), (MLA_HEADS * QK_PAD, BF16), (MLA_HEADS * V_HEAD, BF16),
            (HG_W, F32), (HG_W, BF16), (HG_W, F32), (HG_W, F32), (HG_W, BF16)]
    return pl.pallas_call(
        _in_proj_kernel,
        grid=(t // tm,),
        in_specs=[row(D_MODEL), row(LANES), row(LANES),
                  _const_spec(gmix.shape), _const_spec(w1.shape), _const_spec(gqa.shape),
                  _const_spec(gkva.shape), _const_spec(wq2.shape), _const_spec(wkv2.shape),
                  _const_spec(gq.shape), _const_spec(gk.shape)],
        out_specs=[row(w) for w, _ in outs],
        out_shape=[jax.ShapeDtypeStruct((t, w), dt) for w, dt in outs],
        compiler_params=pltpu.CompilerParams(dimension_semantics=("arbitrary",),
                                             vmem_limit_bytes=VMEM_LIMIT),
        name="in_proj",
    )(x2, cos_t, sin_t, gmix, w1, gqa, gkva, wq2, wkv2, gq, gk)


def _attn_kernel(q_ref, k_ref, v_ref, o_ref, s0_ref, s1_ref, vt_ref, *, tq, tk, unroll, heads):
    seq = k_ref.shape[0]
    nq, nk = seq // tq, seq // tk
    nt = (((1,), (1,)), ((), ()))
    sub = SUBLANES

    def fold(x, op):
        return op(x.reshape(tk // sub, sub, tq), axis=0)

    def q_rows(i):
        start = i * tq
        return pl.ds(start if isinstance(start, int) else pl.multiple_of(start, tq), tq)

    for hd in range(heads):
        vt_ref[hd] = v_ref[:, hd * V_HEAD:(hd + 1) * V_HEAD].astype(F32).T.astype(BF16)

    def step(nxt, s_w, cur, s_r, m_cur):
        if nxt is not None:
            qk_n = slice(nxt[0] * QK_PAD, (nxt[0] + 1) * QK_PAD)
            q = q_ref[q_rows(nxt[1]), qk_n]
            mrun = jnp.full((sub, tq), -jnp.inf, F32)
        if cur is not None:
            lrun = jnp.zeros((sub, tq), F32)
            acc = jnp.zeros((V_HEAD, tq), F32)
        for kc in range(nk):
            ks = slice(kc * tk, (kc + 1) * tk)
            if nxt is not None:
                s_c = lax.dot_general(k_ref[ks, qk_n], q, nt, preferred_element_type=F32)
                s_w[ks, :] = s_c
                mrun = jnp.maximum(mrun, fold(s_c, jnp.max))
            if cur is not None:
                p_c = jnp.exp2(s_r[ks, :].reshape(tk // sub, sub, tq) - m_cur[None]).reshape(tk, tq)
                lrun = lrun + fold(p_c, jnp.sum)
                acc = acc + jnp.dot(vt_ref[cur[0], :, ks], p_c.astype(BF16), preferred_element_type=F32)
        if cur is not None:
            l = jnp.sum(lrun, axis=0, keepdims=True)
            o_ref[q_rows(cur[1]), cur[0] * V_HEAD:(cur[0] + 1) * V_HEAD] = (acc / l).T.astype(o_ref.dtype)
        if nxt is None:
            return None
        return jnp.broadcast_to(jnp.max(mrun, axis=0, keepdims=True), (sub, tq))

    bufs = (s0_ref, s1_ref)
    m = step((0, 0), bufs[0], None, None, None)
    for hd in range(heads):
        def run_steps(first, count, m, hd=hd):
            for u in range(count):
                m = step((hd, first + u + 1), bufs[(u + 1) % 2], (hd, first + u), bufs[u % 2], m)
            return m

        trips = (nq - 1) // unroll
        m = lax.fori_loop(0, trips, lambda j, m, run=run_steps: run(j * unroll, unroll, m), m)
        m = run_steps(trips * unroll, nq - 1 - trips * unroll, m)
        nxt = (hd + 1, 0) if hd + 1 < heads else None
        m = step(nxt, bufs[0], (hd, nq - 1), bufs[1], m)


def _attention(q, k, v, tq, tk, unroll, heads):
    b, s, _ = q.shape
    assert s % (2 * tq) == 0 and s % tk == 0 and tk % LANES == 0 and unroll % 2 == 0
    assert MLA_HEADS % heads == 0
    return pl.pallas_call(
        functools.partial(_attn_kernel, tq=tq, tk=tk, unroll=unroll, heads=heads),
        grid=(b, MLA_HEADS // heads),
        in_specs=[pl.BlockSpec((None, s, heads * QK_PAD), lambda bi, gi: (bi, 0, gi)),
                  pl.BlockSpec((None, s, heads * QK_PAD), lambda bi, gi: (bi, 0, gi)),
                  pl.BlockSpec((None, s, heads * V_HEAD), lambda bi, gi: (bi, 0, gi))],
        out_specs=pl.BlockSpec((None, s, heads * V_HEAD), lambda bi, gi: (bi, 0, gi)),
        out_shape=jax.ShapeDtypeStruct((b, s, MLA_HEADS * V_HEAD), BF16),
        scratch_shapes=[pltpu.VMEM((s, tq), F32), pltpu.VMEM((s, tq), F32),
                        pltpu.VMEM((heads, V_HEAD, s), BF16)],
        compiler_params=pltpu.CompilerParams(dimension_semantics=("arbitrary", "arbitrary"),
                                             vmem_limit_bytes=VMEM_LIMIT),
        name="attention",
    )(q, k, v)


def _hgrn_kernel(qf_ref, vf_ref, ff_ref, qb_ref, vb_ref, fb_ref, lb_ref, tri_ref,
                 of_ref, ob_ref, st_ref, *, n_chunks):
    @pl.when(pl.program_id(1) == 0)
    def _():
        st_ref[...] = jnp.zeros_like(st_ref)

    t_idx = lax.broadcasted_iota(jnp.int32, (CHUNK, CHUNK), 0)
    s_idx = lax.broadcasted_iota(jnp.int32, (CHUNK, CHUNK), 1)
    streams = (
        (qf_ref, vf_ref, ff_ref, None, of_ref, t_idx >= s_idx, CHUNK // 2 - 1, CHUNK - 1),
        (qb_ref, vb_ref, fb_ref, None, ob_ref, t_idx <= s_idx, CHUNK // 2, 0),
    )
    gate_half = [0.5 * (1.0 - lb_ref[d:d + 1, :]) for d in range(2)]
    gate_mid = [0.5 * (1.0 + lb_ref[d:d + 1, :]) for d in range(2)]

    nt = (((1,), (1,)), ((), ()))
    tn = (((0,), (0,)), ((), ()))
    heads = [slice(hd * HG_DK, (hd + 1) * HG_DK) for hd in range(HG_HEADS)]

    def chunk_of(d, i):
        return i if d == 0 else n_chunks - 1 - i

    cum, gated, intra = {}, {}, {}
    state = {(d, hd): st_ref[d, hd] for d in range(2) for hd in range(HG_HEADS)}

    def stage_cumsum(d, i):
        f_ref = streams[d][2]
        rows = slice(chunk_of(d, i) * CHUNK, (chunk_of(d, i) + 1) * CHUNK)
        ct = gate_half[d] * jnp.tanh(0.5 * f_ref[rows, :])
        k = gate_half[d] - ct
        lf = jnp.log2(gate_mid[d] + ct)
        lf_hi = lf.astype(BF16)
        lf_lo = (lf - lf_hi.astype(F32)).astype(BF16)
        hi_lo = jnp.concatenate([lf_hi, lf_lo], axis=0)
        cum[d, i] = (jnp.dot(tri_ref[d], hi_lo, preferred_element_type=F32), k)

    def stage_gates(d, i):
        q_ref, _, _, _, _, _, mid, last = streams[d]
        rows = slice(chunk_of(d, i) * CHUNK, (chunk_of(d, i) + 1) * CHUNK)
        b, k = cum.pop((d, i))
        b_mid = b[mid:mid + 1, :]
        b_last = b[last:last + 1, :]
        x = b - b_mid
        q_in = _silu(q_ref[rows, :]) * jnp.exp2(x)
        k_in = k * jnp.exp2(-x)
        q_st = (q_in * jnp.exp2(b_mid)).astype(BF16)
        k_st = (k_in * jnp.exp2(b_last - b_mid)).astype(BF16)
        gated[d, i] = (q_in.astype(BF16), k_in.astype(BF16), q_st, k_st, jnp.exp2(b_last))

    def stage_intra(d, i):
        v_ref, mask = streams[d][1], streams[d][5]
        rows = slice(chunk_of(d, i) * CHUNK, (chunk_of(d, i) + 1) * CHUNK)
        q_in, k_in, q_st, k_st, decay = gated.pop((d, i))
        v = v_ref[rows, :]
        a, kv_t = [], []
        for sl in heads:
            a_h = lax.dot_general(q_in[:, sl], k_in[:, sl], nt, preferred_element_type=F32)
            a.append(jnp.where(mask, a_h, 0.0).astype(BF16))
            kv_t.append(lax.dot_general(v[:, sl], k_st[:, sl], tn, preferred_element_type=F32))
        intra[d, i] = (rows, v, q_st, a, kv_t, decay)

    def stage_state(d, i):
        o_ref = streams[d][4]
        rows, v, q_st, a, kv_t, decay = intra.pop((d, i))
        for hd, sl in enumerate(heads):
            st = state[d, hd]
            o = jnp.dot(a[hd], v[:, sl], preferred_element_type=F32)
            o += jnp.dot(q_st[:, sl], st.T.astype(BF16), preferred_element_type=F32)
            o_ref[rows, sl] = o
            state[d, hd] = st * decay[:, sl] + kv_t[hd]

    stages = (stage_cumsum, stage_gates, stage_intra, stage_state)
    for t in range(n_chunks + len(stages) - 1):
        for lag, stage in enumerate(stages):
            if 0 <= t - lag < n_chunks:
                for d in range(2):
                    stage(d, t - lag)
    for (d, hd), st in state.items():
        st_ref[d, hd] = st


def _hgrn(hq, hv, hff, hfb, lb, tri, blk):
    b, s, _ = hq.shape
    nb = s // blk
    fwd = pl.BlockSpec((None, blk, HG_W), lambda bi, j: (bi, j, 0))
    bwd = pl.BlockSpec((None, blk, HG_W), lambda bi, j: (bi, nb - 1 - j, 0))
    return pl.pallas_call(
        functools.partial(_hgrn_kernel, n_chunks=blk // CHUNK),
        grid=(b, nb),
        in_specs=[fwd] * 3 + [bwd] * 3 + [_const_spec(lb.shape), _const_spec(tri.shape)],
        out_specs=[fwd, bwd],
        out_shape=[jax.ShapeDtypeStruct((b, s, HG_W), F32)] * 2,
        scratch_shapes=[pltpu.VMEM((2, HG_HEADS, HG_DV, HG_DK), F32)],
        compiler_params=pltpu.CompilerParams(dimension_semantics=("arbitrary", "arbitrary"),
                                             vmem_limit_bytes=VMEM_LIMIT),
        name="hgrn2",
    )(hq, hv, hff, hq, hv, hfb, lb, tri)


def _out_kernel(x_ref, a_ref, of_ref, ob_ref, hg_ref, p_ref, ghgo_ref, woa_ref, wor_ref, gffn_ref,
                wg_ref, wu_ref, wd_ref, gple_ref, wpg_ref, wpp_ref, out_ref, *, ff_chunk):
    o = of_ref[...] + ob_ref[...]
    parts = []
    for hd in range(HG_HEADS):
        oh = o[:, hd * HG_DV:(hd + 1) * HG_DV]
        parts.append(oh * _rms_scale(jnp.sum(oh * oh, axis=-1, keepdims=True), HG_DV))
    r = jnp.concatenate(parts, axis=-1) * ghgo_ref[...] * _silu(hg_ref[...].astype(F32))
    x = (x_ref[...]
         + jnp.dot(a_ref[...], woa_ref[...], preferred_element_type=F32)
         + jnp.dot(r.astype(BF16), wor_ref[...], preferred_element_type=F32))

    h = (x * _rms_scale(jnp.sum(x * x, axis=-1, keepdims=True), D_MODEL) * gffn_ref[...]).astype(BF16)
    ffn = None
    for c in range(D_FF // ff_chunk):
        cs = slice(c * ff_chunk, (c + 1) * ff_chunk)
        gate = jnp.dot(h, wg_ref[:, cs], preferred_element_type=F32)
        up = jnp.dot(h, wu_ref[:, cs], preferred_element_type=F32)
        act = (_silu(gate) * up).astype(BF16)
        down = jnp.dot(act, wd_ref[cs, :], preferred_element_type=F32)
        ffn = down if ffn is None else ffn + down
    x = x + ffn

    h = (x * _rms_scale(jnp.sum(x * x, axis=-1, keepdims=True), D_MODEL) * gple_ref[...]).astype(BF16)
    gate = _sigmoid(jnp.dot(h, wpg_ref[...], preferred_element_type=F32))
    proj = jnp.dot(p_ref[...].astype(BF16), wpp_ref[...], preferred_element_type=F32)
    out_ref[...] = x + gate * proj


def _out_ffn_ple(x2, a, o_f, o_b, hg, p2, ghgo, woa, wor, gffn, wg, wu, wd, gple, wpg, wpp, tm):
    t = x2.shape[0]
    row = lambda w: pl.BlockSpec((tm, w), lambda i: (i, 0))
    consts = (ghgo, woa, wor, gffn, wg, wu, wd, gple, wpg, wpp)
    return pl.pallas_call(
        functools.partial(_out_kernel, ff_chunk=FF_CHUNK),
        grid=(t // tm,),
        in_specs=[row(D_MODEL), row(HG_W), row(HG_W), row(HG_W), row(HG_W), row(PLE_DIM)]
                 + [_const_spec(c.shape) for c in consts],
        out_specs=row(D_MODEL),
        out_shape=jax.ShapeDtypeStruct((t, D_MODEL), F32),
        compiler_params=pltpu.CompilerParams(dimension_semantics=("arbitrary",),
                                             vmem_limit_bytes=VMEM_LIMIT),
        name="out_ffn_ple",
    )(x2, a, o_f, o_b, hg, p2, *consts)


def _swap_halves(w):
    half = w.shape[-1] // 2
    return jnp.concatenate([w[..., half:], w[..., :half]], axis=-1)


def _pad_lanes(w):
    return jnp.pad(w, [(0, 0)] * (w.ndim - 1) + [(0, LANES - w.shape[-1])])


def kernel(x, p, positions, g_mix, w_in, g_qa, g_kva, w_qb, w_kvb, g_qn, g_kn, lb_param, g_hgo,
           w_o, g_ffn, w_gate, w_up, w_down, g_ple, w_ple_gate, w_ple_proj):
    bsz, seq, _ = x.shape
    t = bsz * seq
    layer = 0

    inv_freq = (ROPE_THETA ** (-np.arange(0, QK_ROPE, 2, dtype=np.float64) / QK_ROPE)).astype(np.float32)
    freq_row = jnp.asarray(np.tile(inv_freq, TOK_PER_ROW)[None, :])
    cos_t, sin_t = _rope_tables(positions.reshape(t // TOK_PER_ROW, TOK_PER_ROW), freq_row)

    wi = w_in[layer]
    c_q, c_kv, k_r, h5 = (wi[:, :Q_LORA], wi[:, Q_LORA:Q_LORA + KV_LORA],
                          wi[:, Q_LORA + KV_LORA:Q_LORA + KV_LORA + QK_ROPE],
                          wi[:, Q_LORA + KV_LORA + QK_ROPE:])
    w1 = jnp.concatenate([c_q, c_kv, h5, _pad_lanes(k_r), _pad_lanes(_swap_halves(k_r))], axis=1).astype(BF16)

    wq = w_qb[layer].reshape(Q_LORA, MLA_HEADS, QK_HEAD)
    wq_n = wq[:, :, :QK_NOPE].reshape(Q_LORA, -1)
    wq_r = _pad_lanes(wq[:, :, QK_NOPE:]).reshape(Q_LORA, -1)
    wq_rs = _pad_lanes(_swap_halves(wq[:, :, QK_NOPE:])).reshape(Q_LORA, -1)
    wq2 = jnp.concatenate([wq_n, wq_r, wq_rs], axis=1).astype(BF16)

    wkv = w_kvb[layer].reshape(KV_LORA, MLA_HEADS, QK_NOPE + V_HEAD)
    wkv2 = jnp.concatenate([wkv[:, :, :QK_NOPE].reshape(KV_LORA, -1),
                            wkv[:, :, QK_NOPE:].reshape(KV_LORA, -1)], axis=1).astype(BF16)

    def qk_gain(g):
        g_r = g[QK_NOPE:]
        return jnp.stack([g[:QK_NOPE], _pad_lanes(g_r), _pad_lanes(_swap_halves(g_r))]).astype(F32)

    lower = jax.nn.softmax(lb_param.astype(F32), axis=0)[layer]
    row = lambda g: g.reshape(1, -1).astype(F32)

    x2 = x.reshape(t, D_MODEL)
    q, k, v, hq, hv, hff, hfb, hg = _in_proj(
        x2, cos_t, sin_t, row(g_mix[layer]), w1, row(g_qa[layer]), row(g_kva[layer]), wq2, wkv2,
        qk_gain(g_qn[layer]), qk_gain(g_kn[layer]), tm=IN_PROJ_ROWS)

    a = _attention(q.reshape(bsz, seq, -1), k.reshape(bsz, seq, -1), v.reshape(bsz, seq, -1),
                   tq=ATTN_Q_BLOCK, tk=ATTN_KEY_CHUNK, unroll=ATTN_UNROLL, heads=1)

    tt = np.arange(CHUNK)
    tri = np.stack([tt[:, None] >= tt[None, :], tt[:, None] <= tt[None, :]])
    tri = jnp.asarray(np.concatenate([tri, tri], axis=-1), dtype=BF16)
    r3 = lambda z: z.reshape(bsz, seq, HG_W)
    o_f, o_b = _hgrn(r3(hq), r3(hv), r3(hff), r3(hfb), lower, tri, blk=HGRN_BLOCK)

    wo = w_o[layer].astype(BF16)
    out = _out_ffn_ple(
        x2, a.reshape(t, -1), o_f.reshape(t, HG_W), o_b.reshape(t, HG_W), hg, p[layer].reshape(t, PLE_DIM),
        row(g_hgo[layer]), wo[:MLA_HEADS * V_HEAD], wo[MLA_HEADS * V_HEAD:], row(g_ffn[layer]),
        w_gate[layer].astype(BF16), w_up[layer].astype(BF16), w_down[layer].astype(BF16),
        row(g_ple[layer]), w_ple_gate[layer].astype(BF16), w_ple_proj[layer].astype(BF16),
        tm=OUT_ROWS)
    return out.reshape(bsz, seq, D_MODEL)
```

```python
import functools

import jax
import jax.numpy as jnp
import numpy as np
from jax import lax
from jax.experimental import pallas as pl
from jax.experimental.pallas import tpu as pltpu

D_MODEL = 1024
MLA_HEADS = 4
QK_NOPE = 128
QK_ROPE = 64
V_HEAD = 128
Q_LORA = 256
KV_LORA = 256
QK_HEAD = QK_NOPE + QK_ROPE
HG_HEADS = 4
HG_DK = 128
HG_DV = 128
HG_W = HG_HEADS * HG_DK
CHUNK = 64
D_FF = 2816
PLE_DIM = 256
ROPE_THETA = 10000.0
EPS = 1e-6

LANES = 128
SUBLANES = 8
MXU_TILE = 256
QK_PAD = MXU_TILE
VMEM_LIMIT = 56 * 1024 * 1024

IN_PROJ_ROWS = 512
OUT_ROWS = 512
FF_CHUNK = MXU_TILE
ATTN_Q_BLOCK = MXU_TILE
ATTN_KEY_CHUNK = 2048
ATTN_UNROLL = 6
HGRN_BLOCK = 1024

BF16 = jnp.bfloat16
F32 = jnp.float32


def _const_spec(shape):
    return pl.BlockSpec(shape, lambda *_: (0,) * len(shape), pipeline_mode=pl.Buffered(1))


def _sigmoid(x):
    return 0.5 + 0.5 * jnp.tanh(0.5 * x)


def _silu(x):
    h = 0.5 * x
    return h + h * jnp.tanh(h)


def _rms_scale(ss, n):
    return lax.rsqrt(ss * (1.0 / n) + EPS)


N_FREQ = QK_ROPE // 2
TOK_PER_ROW = LANES // N_FREQ


def _split3(x):
    p1 = x.astype(BF16)
    r1 = x - p1.astype(F32)
    p2 = r1.astype(BF16)
    p3 = (r1 - p2.astype(F32)).astype(BF16)
    return p1, p2, p3


def _rope_kernel(pos_ref, freq_ref, selc_ref, sels_ref, cos_ref, sin_ref):
    rows = pos_ref.shape[0]
    pos = pos_ref[...].astype(F32)
    lane = lax.broadcasted_iota(jnp.int32, (rows, LANES), 1)
    pos_d = jnp.broadcast_to(pos[:, TOK_PER_ROW - 1:TOK_PER_ROW], (rows, LANES))
    for i in range(TOK_PER_ROW - 2, -1, -1):
        pos_d = jnp.where(lane < (i + 1) * N_FREQ, jnp.broadcast_to(pos[:, i:i + 1], (rows, LANES)), pos_d)
    ang = pos_d * freq_ref[...]
    for trig, sel_ref, out_ref in ((jnp.cos(ang), selc_ref, cos_ref), (jnp.sin(ang), sels_ref, sin_ref)):
        pieces = jnp.concatenate(_split3(trig), axis=-1)
        for i in range(TOK_PER_ROW):
            out_ref[pl.ds(i, rows, stride=TOK_PER_ROW), :] = jnp.dot(
                pieces, sel_ref[i], preferred_element_type=F32)


def _rope_select_matrices():
    selc = np.zeros((TOK_PER_ROW, 3 * LANES, LANES), np.float32)
    sels = np.zeros_like(selc)
    j = np.arange(N_FREQ)
    for i in range(TOK_PER_ROW):
        for piece in range(3):
            for m in range(LANES // N_FREQ):
                selc[i, piece * LANES + N_FREQ * i + j, N_FREQ * m + j] = 1.0
                sels[i, piece * LANES + N_FREQ * i + j, N_FREQ * m + j] = 1.0 if m % 2 else -1.0
    return jnp.asarray(selc, BF16), jnp.asarray(sels, BF16)


def _rope_tables(pos4, freq_row):
    rows = pos4.shape[0]
    blk = 1024
    selc, sels = _rope_select_matrices()
    out = pl.BlockSpec((blk * TOK_PER_ROW, LANES), lambda i: (i, 0))
    return pl.pallas_call(
        _rope_kernel,
        grid=(rows // blk,),
        in_specs=[pl.BlockSpec((blk, TOK_PER_ROW), lambda i: (i, 0)), _const_spec(freq_row.shape),
                  _const_spec(selc.shape), _const_spec(sels.shape)],
        out_specs=[out, out],
        out_shape=[jax.ShapeDtypeStruct((rows * TOK_PER_ROW, LANES), F32)] * 2,
        compiler_params=pltpu.CompilerParams(dimension_semantics=("arbitrary",)),
        name="rope_tables",
    )(pos4, freq_row, selc, sels)


def _in_proj_kernel(x_ref, cos_ref, sin_ref, gmix_ref, w1_ref, gqa_ref, gkva_ref, wq2_ref, wkv2_ref,
                    gq_ref, gk_ref,
                    q_ref, k_ref, v_ref, hq_ref, hv_ref, hff_ref, hfb_ref, hg_ref):
    x = x_ref[...]
    h = x * _rms_scale(jnp.sum(x * x, axis=-1, keepdims=True), D_MODEL) * gmix_ref[...]
    z = jnp.dot(h.astype(BF16), w1_ref[...], preferred_element_type=F32)

    c_q = z[:, 0:Q_LORA]
    c_kv = z[:, Q_LORA:Q_LORA + KV_LORA]
    o = Q_LORA + KV_LORA
    hq = z[:, o:o + HG_W]
    hff = z[:, o + HG_W:o + 2 * HG_W]
    hfb = z[:, o + 2 * HG_W:o + 3 * HG_W]
    hi = z[:, o + 3 * HG_W:o + 4 * HG_W]
    hg = z[:, o + 4 * HG_W:o + 5 * HG_W]
    o += 5 * HG_W
    k_r = z[:, o:o + LANES]
    k_rs = z[:, o + LANES:o + 2 * LANES]

    cos = cos_ref[...]
    sin = sin_ref[...]

    cqn = c_q * _rms_scale(jnp.sum(c_q * c_q, axis=-1, keepdims=True), Q_LORA) * gqa_ref[...]
    q_all = jnp.dot(cqn.astype(BF16), wq2_ref[...], preferred_element_type=F32)
    gq_n, gq_r, gq_rs = gq_ref[0:1, :], gq_ref[1:2, :], gq_ref[2:3, :]
    scale = QK_HEAD ** -0.5 * np.log2(np.e)
    nh = MLA_HEADS
    for hd in range(nh):
        qn = q_all[:, hd * LANES:(hd + 1) * LANES]
        qr = q_all[:, (nh + hd) * LANES:(nh + hd + 1) * LANES]
        qrs = q_all[:, (2 * nh + hd) * LANES:(2 * nh + hd + 1) * LANES]
        ss = jnp.sum(qn * qn + qr * qr, axis=-1, keepdims=True)
        r = _rms_scale(ss, QK_HEAD) * scale
        q_ref[:, hd * QK_PAD:hd * QK_PAD + LANES] = (qn * gq_n * r).astype(BF16)
        q_ref[:, hd * QK_PAD + LANES:(hd + 1) * QK_PAD] = (
            (qr * gq_r * cos + qrs * gq_rs * sin) * r).astype(BF16)

    ckvn = c_kv * _rms_scale(jnp.sum(c_kv * c_kv, axis=-1, keepdims=True), KV_LORA) * gkva_ref[...]
    kv_all = jnp.dot(ckvn.astype(BF16), wkv2_ref[...], preferred_element_type=F32)
    gk_n, gk_r, gk_rs = gk_ref[0:1, :], gk_ref[1:2, :], gk_ref[2:3, :]
    k_rot = k_r * gk_r * cos + k_rs * gk_rs * sin
    ss_r = jnp.sum(k_r * k_r, axis=-1, keepdims=True)
    for hd in range(nh):
        kn = kv_all[:, hd * LANES:(hd + 1) * LANES]
        ss = jnp.sum(kn * kn, axis=-1, keepdims=True) + ss_r
        r = _rms_scale(ss, QK_HEAD)
        k_ref[:, hd * QK_PAD:hd * QK_PAD + LANES] = (kn * gk_n * r).astype(BF16)
        k_ref[:, hd * QK_PAD + LANES:(hd + 1) * QK_PAD] = (k_rot * r).astype(BF16)
    v_ref[...] = kv_all[:, nh * LANES:].astype(BF16)

    hq_ref[...] = hq
    hv_ref[...] = hi.astype(BF16)
    hff_ref[...] = hff
    hfb_ref[...] = hfb
    hg_ref[...] = _silu(hg).astype(BF16)


def _in_proj(x2, cos_t, sin_t, gmix, w1, gqa, gkva, wq2, wkv2, gq, gk, tm):
    t = x2.shape[0]
    row = lambda w: pl.BlockSpec((tm, w), lambda i: (i, 0))
    outs = [(MLA_HEADS * QK_PAD, BF16), (MLA_HEADS * QK_PAD, BF16), (MLA_HEADS * V_HEAD, BF16),
            (HG_W, F32), (HG_W, BF16), (HG_W, F32), (HG_W, F32), (HG_W, BF16)]
    return pl.pallas_call(
        _in_proj_kernel,
        grid=(t // tm,),
        in_specs=[row(D_MODEL), row(LANES), row(LANES),
                  _const_spec(gmix.shape), _const_spec(w1.shape), _const_spec(gqa.shape),
                  _const_spec(gkva.shape), _const_spec(wq2.shape), _const_spec(wkv2.shape),
                  _const_spec(gq.shape), _const_spec(gk.shape)],
        out_specs=[row(w) for w, _ in outs],
        out_shape=[jax.ShapeDtypeStruct((t, w), dt) for w, dt in outs],
        compiler_params=pltpu.CompilerParams(dimension_semantics=("arbitrary",),
                                             vmem_limit_bytes=VMEM_LIMIT),
        name="in_proj",
    )(x2, cos_t, sin_t, gmix, w1, gqa, gkva, wq2, wkv2, gq, gk)


def _attn_kernel(q_ref, k_ref, v_ref, o_ref, s0_ref, s1_ref, vt_ref, *, tq, tk, unroll, heads):
    seq = k_ref.shape[0]
    nq, nk = seq // tq, seq // tk
    nt = (((1,), (1,)), ((), ()))
    sub = SUBLANES

    def fold(x, op):
        return op(x.reshape(tk // sub, sub, tq), axis=0)

    def q_rows(i):
        start = i * tq
        return pl.ds(start if isinstance(start, int) else pl.multiple_of(start, tq), tq)

    for hd in range(heads):
        vt_ref[hd] = v_ref[:, hd * V_HEAD:(hd + 1) * V_HEAD].astype(F32).T.astype(BF16)

    def step(nxt, s_w, cur, s_r, m_cur):
        if nxt is not None:
            qk_n = slice(nxt[0] * QK_PAD, (nxt[0] + 1) * QK_PAD)
            q = q_ref[q_rows(nxt[1]), qk_n]
            mrun = jnp.full((sub, tq), -jnp.inf, F32)
        if cur is not None:
            lrun = jnp.zeros((sub, tq), F32)
            acc = jnp.zeros((V_HEAD, tq), F32)
        for kc in range(nk):
            ks = slice(kc * tk, (kc + 1) * tk)
            if nxt is not None:
                s_c = lax.dot_general(k_ref[ks, qk_n], q, nt, preferred_element_type=F32)
                s_w[ks, :] = s_c
                mrun = jnp.maximum(mrun, fold(s_c, jnp.max))
            if cur is not None:
                p_c = jnp.exp2(s_r[ks, :].reshape(tk // sub, sub, tq) - m_cur[None]).reshape(tk, tq)
                lrun = lrun + fold(p_c, jnp.sum)
                acc = acc + jnp.dot(vt_ref[cur[0], :, ks], p_c.astype(BF16), preferred_element_type=F32)
        if cur is not None:
            l = jnp.sum(lrun, axis=0, keepdims=True)
            o_ref[q_rows(cur[1]), cur[0] * V_HEAD:(cur[0] + 1) * V_HEAD] = (acc / l).T.astype(o_ref.dtype)
        if nxt is None:
            return None
        return jnp.broadcast_to(jnp.max(mrun, axis=0, keepdims=True), (sub, tq))

    bufs = (s0_ref, s1_ref)
    m = step((0, 0), bufs[0], None, None, None)
    for hd in range(heads):
        def run_steps(first, count, m, hd=hd):
            for u in range(count):
                m = step((hd, first + u + 1), bufs[(u + 1) % 2], (hd, first + u), bufs[u % 2], m)
            return m

        trips = (nq - 1) // unroll
        m = lax.fori_loop(0, trips, lambda j, m, run=run_steps: run(j * unroll, unroll, m), m)
        m = run_steps(trips * unroll, nq - 1 - trips * unroll, m)
        nxt = (hd + 1, 0) if hd + 1 < heads else None
        m = step(nxt, bufs[0], (hd, nq - 1), bufs[1], m)


def _attention(q, k, v, tq, tk, unroll, heads):
    b, s, _ = q.shape
    assert s % (2 * tq) == 0 and s % tk == 0 and tk % LANES == 0 and unroll % 2 == 0
    assert MLA_HEADS % heads == 0
    return pl.pallas_call(
        functools.partial(_attn_kernel, tq=tq, tk=tk, unroll=unroll, heads=heads),
        grid=(b, MLA_HEADS // heads),
        in_specs=[pl.BlockSpec((None, s, heads * QK_PAD), lambda bi, gi: (bi, 0, gi)),
                  pl.BlockSpec((None, s, heads * QK_PAD), lambda bi, gi: (bi, 0, gi)),
                  pl.BlockSpec((None, s, heads * V_HEAD), lambda bi, gi: (bi, 0, gi))],
        out_specs=pl.BlockSpec((None, s, heads * V_HEAD), lambda bi, gi: (bi, 0, gi)),
        out_shape=jax.ShapeDtypeStruct((b, s, MLA_HEADS * V_HEAD), BF16),
        scratch_shapes=[pltpu.VMEM((s, tq), F32), pltpu.VMEM((s, tq), F32),
                        pltpu.VMEM((heads, V_HEAD, s), BF16)],
        compiler_params=pltpu.CompilerParams(dimension_semantics=("arbitrary", "arbitrary"),
                                             vmem_limit_bytes=VMEM_LIMIT),
        name="attention",
    )(q, k, v)


def _hgrn_kernel(qf_ref, vf_ref, ff_ref, qb_ref, vb_ref, fb_ref, lb_ref, tri_ref,
                 of_ref, ob_ref, st_ref, *, n_chunks):
    @pl.when(pl.program_id(1) == 0)
    def _():
        st_ref[...] = jnp.zeros_like(st_ref)

    t_idx = lax.broadcasted_iota(jnp.int32, (CHUNK, CHUNK), 0)
    s_idx = lax.broadcasted_iota(jnp.int32, (CHUNK, CHUNK), 1)
    streams = (
        (qf_ref, vf_ref, ff_ref, None, of_ref, t_idx >= s_idx, CHUNK // 2 - 1, CHUNK - 1),
        (qb_ref, vb_ref, fb_ref, None, ob_ref, t_idx <= s_idx, CHUNK // 2, 0),
    )
    gate_half = [0.5 * (1.0 - lb_ref[d:d + 1, :]) for d in range(2)]
    gate_mid = [0.5 * (1.0 + lb_ref[d:d + 1, :]) for d in range(2)]

    nt = (((1,), (1,)), ((), ()))
    tn = (((0,), (0,)), ((), ()))
    heads = [slice(hd * HG_DK, (hd + 1) * HG_DK) for hd in range(HG_HEADS)]

    def chunk_of(d, i):
        return i if d == 0 else n_chunks - 1 - i

    cum, gated, intra = {}, {}, {}
    state = {(d, hd): st_ref[d, hd] for d in range(2) for hd in range(HG_HEADS)}

    def stage_cumsum(d, i):
        f_ref = streams[d][2]
        rows = slice(chunk_of(d, i) * CHUNK, (chunk_of(d, i) + 1) * CHUNK)
        ct = gate_half[d] * jnp.tanh(0.5 * f_ref[rows, :])
        k = gate_half[d] - ct
        lf = jnp.log2(gate_mid[d] + ct)
        lf_hi = lf.astype(BF16)
        lf_lo = (lf - lf_hi.astype(F32)).astype(BF16)
        hi_lo = jnp.concatenate([lf_hi, lf_lo], axis=0)
        cum[d, i] = (jnp.dot(tri_ref[d], hi_lo, preferred_element_type=F32), k)

    def stage_gates(d, i):
        q_ref, _, _, _, _, _, mid, last = streams[d]
        rows = slice(chunk_of(d, i) * CHUNK, (chunk_of(d, i) + 1) * CHUNK)
        b, k = cum.pop((d, i))
        b_mid = b[mid:mid + 1, :]
        b_last = b[last:last + 1, :]
        x = b - b_mid
        q_in = _silu(q_ref[rows, :]) * jnp.exp2(x)
        k_in = k * jnp.exp2(-x)
        q_st = (q_in * jnp.exp2(b_mid)).astype(BF16)
        k_st = (k_in * jnp.exp2(b_last - b_mid)).astype(BF16)
        gated[d, i] = (q_in.astype(BF16), k_in.astype(BF16), q_st, k_st, jnp.exp2(b_last))

    def stage_intra(d, i):
        v_ref, mask = streams[d][1], streams[d][5]
        rows = slice(chunk_of(d, i) * CHUNK, (chunk_of(d, i) + 1) * CHUNK)
        q_in, k_in, q_st, k_st, decay = gated.pop((d, i))
        v = v_ref[rows, :]
        a, kv_t = [], []
        for sl in heads:
            a_h = lax.dot_general(q_in[:, sl], k_in[:, sl], nt, preferred_element_type=F32)
            a.append(jnp.where(mask, a_h, 0.0).astype(BF16))
            kv_t.append(lax.dot_general(v[:, sl], k_st[:, sl], tn, preferred_element_type=F32))
        intra[d, i] = (rows, v, q_st, a, kv_t, decay)

    def stage_state(d, i):
        o_ref = streams[d][4]
        rows, v, q_st, a, kv_t, decay = intra.pop((d, i))
        for hd, sl in enumerate(heads):
            st = state[d, hd]
            o = jnp.dot(a[hd], v[:, sl], preferred_element_type=F32)
            o += jnp.dot(q_st[:, sl], st.T.astype(BF16), preferred_element_type=F32)
            o_ref[rows, sl] = o
            state[d, hd] = st * decay[:, sl] + kv_t[hd]

    stages = (stage_cumsum, stage_gates, stage_intra, stage_state)
    for t in range(n_chunks + len(stages) - 1):
        for lag, stage in enumerate(stages):
            if 0 <= t - lag < n_chunks:
                for d in range(2):
                    stage(d, t - lag)
    for (d, hd), st in state.items():
        st_ref[d, hd] = st


def _hgrn(hq, hv, hff, hfb, lb, tri, blk):
    b, s, _ = hq.shape
    nb = s // blk
    fwd = pl.BlockSpec((None, blk, HG_W), lambda bi, j: (bi, j, 0))
    bwd = pl.BlockSpec((None, blk, HG_W), lambda bi, j: (bi, nb - 1 - j, 0))
    return pl.pallas_call(
        functools.partial(_hgrn_kernel, n_chunks=blk // CHUNK),
        grid=(b, nb),
        in_specs=[fwd] * 3 + [bwd] * 3 + [_const_spec(lb.shape), _const_spec(tri.shape)],
        out_specs=[fwd, bwd],
        out_shape=[jax.ShapeDtypeStruct((b, s, HG_W), F32)] * 2,
        scratch_shapes=[pltpu.VMEM((2, HG_HEADS, HG_DV, HG_DK), F32)],
        compiler_params=pltpu.CompilerParams(dimension_semantics=("arbitrary", "arbitrary"),
                                             vmem_limit_bytes=VMEM_LIMIT),
        name="hgrn2",
    )(hq, hv, hff, hq, hv, hfb, lb, tri)


def _out_kernel(x_ref, a_ref, of_ref, ob_ref, hg_ref, p_ref, ghgo_ref, woa_ref, wor_ref, gffn_ref,
                wg_ref, wu_ref, wd_ref, gple_ref, wpg_ref, wpp_ref, out_ref, *, ff_chunk):
    o = of_ref[...] + ob_ref[...]
    parts = []
    for hd in range(HG_HEADS):
        oh = o[:, hd * HG_DV:(hd + 1) * HG_DV]
        parts.append(oh * _rms_scale(jnp.sum(oh * oh, axis=-1, keepdims=True), HG_DV))
    r = jnp.concatenate(parts, axis=-1) * ghgo_ref[...] * hg_ref[...].astype(F32)
    x = (x_ref[...]
         + jnp.dot(a_ref[...], woa_ref[...], preferred_element_type=F32)
         + jnp.dot(r.astype(BF16), wor_ref[...], preferred_element_type=F32))

    h = (x * _rms_scale(jnp.sum(x * x, axis=-1, keepdims=True), D_MODEL) * gffn_ref[...]).astype(BF16)
    ffn = None
    for c in range(D_FF // ff_chunk):
        cs = slice(c * ff_chunk, (c + 1) * ff_chunk)
        gate = jnp.dot(h, wg_ref[:, cs], preferred_element_type=F32)
        up = jnp.dot(h, wu_ref[:, cs], preferred_element_type=F32)
        act = (_silu(gate) * up).astype(BF16)
        down = jnp.dot(act, wd_ref[cs, :], preferred_element_type=F32)
        ffn = down if ffn is None else ffn + down
    x = x + ffn

    h = (x * _rms_scale(jnp.sum(x * x, axis=-1, keepdims=True), D_MODEL) * gple_ref[...]).astype(BF16)
    gate = _sigmoid(jnp.dot(h, wpg_ref[...], preferred_element_type=F32))
    proj = jnp.dot(p_ref[...].astype(BF16), wpp_ref[...], preferred_element_type=F32)
    out_ref[...] = x + gate * proj


def _out_ffn_ple(x2, a, o_f, o_b, hg, p2, ghgo, woa, wor, gffn, wg, wu, wd, gple, wpg, wpp, tm):
    t = x2.shape[0]
    row = lambda w: pl.BlockSpec((tm, w), lambda i: (i, 0))
    consts = (ghgo, woa, wor, gffn, wg, wu, wd, gple, wpg, wpp)
    return pl.pallas_call(
        functools.partial(_out_kernel, ff_chunk=FF_CHUNK),
        grid=(t // tm,),
        in_specs=[row(D_MODEL), row(HG_W), row(HG_W), row(HG_W), row(HG_W), row(PLE_DIM)]
                 + [_const_spec(c.shape) for c in consts],
        out_specs=row(D_MODEL),
        out_shape=jax.ShapeDtypeStruct((t, D_MODEL), F32),
        compiler_params=pltpu.CompilerParams(dimension_semantics=("arbitrary",),
                                             vmem_limit_bytes=VMEM_LIMIT),
        name="out_ffn_ple",
    )(x2, a, o_f, o_b, hg, p2, *consts)


def _swap_halves(w):
    half = w.shape[-1] // 2
    return jnp.concatenate([w[..., half:], w[..., :half]], axis=-1)


def _pad_lanes(w):
    return jnp.pad(w, [(0, 0)] * (w.ndim - 1) + [(0, LANES - w.shape[-1])])


def kernel(x, p, positions, g_mix, w_in, g_qa, g_kva, w_qb, w_kvb, g_qn, g_kn, lb_param, g_hgo,
           w_o, g_ffn, w_gate, w_up, w_down, g_ple, w_ple_gate, w_ple_proj):
    bsz, seq, _ = x.shape
    t = bsz * seq
    layer = 0

    inv_freq = (ROPE_THETA ** (-np.arange(0, QK_ROPE, 2, dtype=np.float64) / QK_ROPE)).astype(np.float32)
    freq_row = jnp.asarray(np.tile(inv_freq, TOK_PER_ROW)[None, :])
    cos_t, sin_t = _rope_tables(positions.reshape(t // TOK_PER_ROW, TOK_PER_ROW), freq_row)

    wi = w_in[layer]
    c_q, c_kv, k_r, h5 = (wi[:, :Q_LORA], wi[:, Q_LORA:Q_LORA + KV_LORA],
                          wi[:, Q_LORA + KV_LORA:Q_LORA + KV_LORA + QK_ROPE],
                          wi[:, Q_LORA + KV_LORA + QK_ROPE:])
    w1 = jnp.concatenate([c_q, c_kv, h5, _pad_lanes(k_r), _pad_lanes(_swap_halves(k_r))], axis=1).astype(BF16)

    wq = w_qb[layer].reshape(Q_LORA, MLA_HEADS, QK_HEAD)
    wq_n = wq[:, :, :QK_NOPE].reshape(Q_LORA, -1)
    wq_r = _pad_lanes(wq[:, :, QK_NOPE:]).reshape(Q_LORA, -1)
    wq_rs = _pad_lanes(_swap_halves(wq[:, :, QK_NOPE:])).reshape(Q_LORA, -1)
    wq2 = jnp.concatenate([wq_n, wq_r, wq_rs], axis=1).astype(BF16)

    wkv = w_kvb[layer].reshape(KV_LORA, MLA_HEADS, QK_NOPE + V_HEAD)
    wkv2 = jnp.concatenate([wkv[:, :, :QK_NOPE].reshape(KV_LORA, -1),
                            wkv[:, :, QK_NOPE:].reshape(KV_LORA, -1)], axis=1).astype(BF16)

    def qk_gain(g):
        g_r = g[QK_NOPE:]
        return jnp.stack([g[:QK_NOPE], _pad_lanes(g_r), _pad_lanes(_swap_halves(g_r))]).astype(F32)

    lower = jax.nn.softmax(lb_param.astype(F32), axis=0)[layer]
    row = lambda g: g.reshape(1, -1).astype(F32)

    x2 = x.reshape(t, D_MODEL)
    q, k, v, hq, hv, hff, hfb, hg = _in_proj(
        x2, cos_t, sin_t, row(g_mix[layer]), w1, row(g_qa[layer]), row(g_kva[layer]), wq2, wkv2,
        qk_gain(g_qn[layer]), qk_gain(g_kn[layer]), tm=IN_PROJ_ROWS)

    a = _attention(q.reshape(bsz, seq, -1), k.reshape(bsz, seq, -1), v.reshape(bsz, seq, -1),
                   tq=ATTN_Q_BLOCK, tk=ATTN_KEY_CHUNK, unroll=ATTN_UNROLL, heads=1)

    tt = np.arange(CHUNK)
    tri = np.stack([tt[:, None] >= tt[None, :], tt[:, None] <= tt[None, :]])
    tri = jnp.asarray(np.concatenate([tri, tri], axis=-1), dtype=BF16)
    r3 = lambda z: z.reshape(bsz, seq, HG_W)
    o_f, o_b = _hgrn(r3(hq), r3(hv), r3(hff), r3(hfb), lower, tri, blk=HGRN_BLOCK)

    wo = w_o[layer].astype(BF16)
    out = _out_ffn_ple(
        x2, a.reshape(t, -1), o_f.reshape(t, HG_W), o_b.reshape(t, HG_W), hg, p[layer].reshape(t, PLE_DIM),
        row(g_hgo[layer]), wo[:MLA_HEADS * V_HEAD], wo[MLA_HEADS * V_HEAD:], row(g_ffn[layer]),
        w_gate[layer].astype(BF16), w_up[layer].astype(BF16), w_down[layer].astype(BF16),
        row(g_ple[layer]), w_ple_gate[layer].astype(BF16), w_ple_proj[layer].astype(BF16),
        tm=OUT_ROWS)
    return out.reshape(bsz, seq, D_MODEL)
```

```python
import functools

import jax
import jax.numpy as jnp
import numpy as np
from jax import lax
from jax.experimental import pallas as pl
from jax.experimental.pallas import tpu as pltpu

D_MODEL = 1024
MLA_HEADS = 4
QK_NOPE = 128
QK_ROPE = 64
V_HEAD = 128
Q_LORA = 256
KV_LORA = 256
QK_HEAD = QK_NOPE + QK_ROPE
HG_HEADS = 4
HG_DK = 128
HG_DV = 128
HG_W = HG_HEADS * HG_DK
CHUNK = 64
D_FF = 2816
PLE_DIM = 256
ROPE_THETA = 10000.0
EPS = 1e-6

LANES = 128
SUBLANES = 8
MXU_TILE = 256
QK_PAD = MXU_TILE
VMEM_LIMIT = 56 * 1024 * 1024

IN_PROJ_ROWS = 512
OUT_ROWS = 512
FF_CHUNK = MXU_TILE
ATTN_Q_BLOCK = MXU_TILE
ATTN_KEY_CHUNK = 2048
ATTN_UNROLL = 6
HGRN_BLOCK = 1024

BF16 = jnp.bfloat16
F32 = jnp.float32


def _const_spec(shape):
    return pl.BlockSpec(shape, lambda *_: (0,) * len(shape), pipeline_mode=pl.Buffered(1))


def _sigmoid(x):
    return 0.5 + 0.5 * jnp.tanh(0.5 * x)


def _silu(x):
    h = 0.5 * x
    return h + h * jnp.tanh(h)


def _rms_scale(ss, n):
    return lax.rsqrt(ss * (1.0 / n) + EPS)


N_FREQ = QK_ROPE // 2
TOK_PER_ROW = LANES // N_FREQ


def _split3(x):
    p1 = x.astype(BF16)
    r1 = x - p1.astype(F32)
    p2 = r1.astype(BF16)
    p3 = (r1 - p2.astype(F32)).astype(BF16)
    return p1, p2, p3


def _rope_kernel(pos_ref, freq_ref, selc_ref, sels_ref, cos_ref, sin_ref):
    rows = pos_ref.shape[0]
    pos = pos_ref[...].astype(F32)
    lane = lax.broadcasted_iota(jnp.int32, (rows, LANES), 1)
    pos_d = jnp.broadcast_to(pos[:, TOK_PER_ROW - 1:TOK_PER_ROW], (rows, LANES))
    for i in range(TOK_PER_ROW - 2, -1, -1):
        pos_d = jnp.where(lane < (i + 1) * N_FREQ, jnp.broadcast_to(pos[:, i:i + 1], (rows, LANES)), pos_d)
    ang = pos_d * freq_ref[...]
    for trig, sel_ref, out_ref in ((jnp.cos(ang), selc_ref, cos_ref), (jnp.sin(ang), sels_ref, sin_ref)):
        pieces = jnp.concatenate(_split3(trig), axis=-1)
        for i in range(TOK_PER_ROW):
            out_ref[pl.ds(i, rows, stride=TOK_PER_ROW), :] = jnp.dot(
                pieces, sel_ref[i], preferred_element_type=F32)


def _rope_select_matrices():
    selc = np.zeros((TOK_PER_ROW, 3 * LANES, LANES), np.float32)
    sels = np.zeros_like(selc)
    j = np.arange(N_FREQ)
    for i in range(TOK_PER_ROW):
        for piece in range(3):
            for m in range(LANES // N_FREQ):
                selc[i, piece * LANES + N_FREQ * i + j, N_FREQ * m + j] = 1.0
                sels[i, piece * LANES + N_FREQ * i + j, N_FREQ * m + j] = 1.0 if m % 2 else -1.0
    return jnp.asarray(selc, BF16), jnp.asarray(sels, BF16)


def _rope_tables(pos4, freq_row):
    rows = pos4.shape[0]
    blk = 1024
    selc, sels = _rope_select_matrices()
    out = pl.BlockSpec((blk * TOK_PER_ROW, LANES), lambda i: (i, 0))
    return pl.pallas_call(
        _rope_kernel,
        grid=(rows // blk,),
        in_specs=[pl.BlockSpec((blk, TOK_PER_ROW), lambda i: (i, 0)), _const_spec(freq_row.shape),
                  _const_spec(selc.shape), _const_spec(sels.shape)],
        out_specs=[out, out],
        out_shape=[jax.ShapeDtypeStruct((rows * TOK_PER_ROW, LANES), F32)] * 2,
        compiler_params=pltpu.CompilerParams(dimension_semantics=("arbitrary",)),
        name="rope_tables",
    )(pos4, freq_row, selc, sels)


def _in_proj_kernel(x_ref, cos_ref, sin_ref, gmix_ref, w1_ref, gqa_ref, gkva_ref, wq2_ref, wkv2_ref,
                    gq_ref, gk_ref,
                    q_ref, k_ref, v_ref, hq_ref, hv_ref, hff_ref, hfb_ref, hg_ref):
    x = x_ref[...]
    h = x * _rms_scale(jnp.sum(x * x, axis=-1, keepdims=True), D_MODEL) * gmix_ref[...]
    z = jnp.dot(h.astype(BF16), w1_ref[...], preferred_element_type=F32)

    c_q = z[:, 0:Q_LORA]
    c_kv = z[:, Q_LORA:Q_LORA + KV_LORA]
    o = Q_LORA + KV_LORA
    hq = z[:, o:o + HG_W]
    hff = z[:, o + HG_W:o + 2 * HG_W]
    hfb = z[:, o + 2 * HG_W:o + 3 * HG_W]
    hi = z[:, o + 3 * HG_W:o + 4 * HG_W]
    hg = z[:, o + 4 * HG_W:o + 5 * HG_W]
    o += 5 * HG_W
    k_r = z[:, o:o + LANES]
    k_rs = z[:, o + LANES:o + 2 * LANES]

    cos = cos_ref[...]
    sin = sin_ref[...]

    cqn = c_q * _rms_scale(jnp.sum(c_q * c_q, axis=-1, keepdims=True), Q_LORA) * gqa_ref[...]
    q_all = jnp.dot(cqn.astype(BF16), wq2_ref[...], preferred_element_type=F32)
    gq_n, gq_r, gq_rs = gq_ref[0:1, :], gq_ref[1:2, :], gq_ref[2:3, :]
    scale = QK_HEAD ** -0.5 * np.log2(np.e)
    nh = MLA_HEADS
    for hd in range(nh):
        qn = q_all[:, hd * LANES:(hd + 1) * LANES]
        qr = q_all[:, (nh + hd) * LANES:(nh + hd + 1) * LANES]
        qrs = q_all[:, (2 * nh + hd) * LANES:(2 * nh + hd + 1) * LANES]
        ss = jnp.sum(qn * qn + qr * qr, axis=-1, keepdims=True)
        r = _rms_scale(ss, QK_HEAD) * scale
        q_ref[:, hd * QK_PAD:hd * QK_PAD + LANES] = (qn * gq_n * r).astype(BF16)
        q_ref[:, hd * QK_PAD + LANES:(hd + 1) * QK_PAD] = (
            (qr * gq_r * cos + qrs * gq_rs * sin) * r).astype(BF16)

    ckvn = c_kv * _rms_scale(jnp.sum(c_kv * c_kv, axis=-1, keepdims=True), KV_LORA) * gkva_ref[...]
    kv_all = jnp.dot(ckvn.astype(BF16), wkv2_ref[...], preferred_element_type=F32)
    gk_n, gk_r, gk_rs = gk_ref[0:1, :], gk_ref[1:2, :], gk_ref[2:3, :]
    k_rot = k_r * gk_r * cos + k_rs * gk_rs * sin
    ss_r = jnp.sum(k_r * k_r, axis=-1, keepdims=True)
    for hd in range(nh):
        kn = kv_all[:, hd * LANES:(hd + 1) * LANES]
        ss = jnp.sum(kn * kn, axis=-1, keepdims=True) + ss_r
        r = _rms_scale(ss, QK_HEAD)
        k_ref[:, hd * QK_PAD:hd * QK_PAD + LANES] = (kn * gk_n * r).astype(BF16)
        k_ref[:, hd * QK_PAD + LANES:(hd + 1) * QK_PAD] = (k_rot * r).astype(BF16)
    v_ref[...] = kv_all[:, nh * LANES:].astype(BF16)

    hq_ref[...] = _silu(hq)
    hv_ref[...] = hi.astype(BF16)
    hff_ref[...] = hff
    hfb_ref[...] = hfb
    hg_ref[...] = _silu(hg).astype(BF16)


def _in_proj(x2, cos_t, sin_t, gmix, w1, gqa, gkva, wq2, wkv2, gq, gk, tm):
    t = x2.shape[0]
    row = lambda w: pl.BlockSpec((tm, w), lambda i: (i, 0))
    outs = [(MLA_HEADS * QK_PAD, BF16), (MLA_HEADS * QK_PAD, BF16), (MLA_HEADS * V_HEAD, BF16),
            (HG_W, F32), (HG_W, BF16), (HG_W, F32), (HG_W, F32), (HG_W, BF16)]
    return pl.pallas_call(
        _in_proj_kernel,
        grid=(t // tm,),
        in_specs=[row(D_MODEL), row(LANES), row(LANES),
                  _const_spec(gmix.shape), _const_spec(w1.shape), _const_spec(gqa.shape),
                  _const_spec(gkva.shape), _const_spec(wq2.shape), _const_spec(wkv2.shape),
                  _const_spec(gq.shape), _const_spec(gk.shape)],
        out_specs=[row(w) for w, _ in outs],
        out_shape=[jax.ShapeDtypeStruct((t, w), dt) for w, dt in outs],
        compiler_params=pltpu.CompilerParams(dimension_semantics=("arbitrary",),
                                             vmem_limit_bytes=VMEM_LIMIT),
        name="in_proj",
    )(x2, cos_t, sin_t, gmix, w1, gqa, gkva, wq2, wkv2, gq, gk)


def _attn_kernel(q_ref, k_ref, v_ref, o_ref, s0_ref, s1_ref, vt_ref, *, tq, tk, unroll, heads):
    seq = k_ref.shape[0]
    nq, nk = seq // tq, seq // tk
    nt = (((1,), (1,)), ((), ()))
    sub = SUBLANES

    def fold(x, op):
        return op(x.reshape(tk // sub, sub, tq), axis=0)

    def q_rows(i):
        start = i * tq
        return pl.ds(start if isinstance(start, int) else pl.multiple_of(start, tq), tq)

    for hd in range(heads):
        vt_ref[hd] = v_ref[:, hd * V_HEAD:(hd + 1) * V_HEAD].astype(F32).T.astype(BF16)

    def step(nxt, s_w, cur, s_r, m_cur):
        if nxt is not None:
            qk_n = slice(nxt[0] * QK_PAD, (nxt[0] + 1) * QK_PAD)
            q = q_ref[q_rows(nxt[1]), qk_n]
            mrun = jnp.full((sub, tq), -jnp.inf, F32)
        if cur is not None:
            lrun = jnp.zeros((sub, tq), F32)
            acc = jnp.zeros((V_HEAD, tq), F32)
        for kc in range(nk):
            ks = slice(kc * tk, (kc + 1) * tk)
            if nxt is not None:
                s_c = lax.dot_general(k_ref[ks, qk_n], q, nt, preferred_element_type=F32)
                s_w[ks, :] = s_c
                mrun = jnp.maximum(mrun, fold(s_c, jnp.max))
            if cur is not None:
                p_c = jnp.exp2(s_r[ks, :].reshape(tk // sub, sub, tq) - m_cur[None]).reshape(tk, tq)
                lrun = lrun + fold(p_c, jnp.sum)
                acc = acc + jnp.dot(vt_ref[cur[0], :, ks], p_c.astype(BF16), preferred_element_type=F32)
        if cur is not None:
            l = jnp.sum(lrun, axis=0, keepdims=True)
            o_ref[q_rows(cur[1]), cur[0] * V_HEAD:(cur[0] + 1) * V_HEAD] = (acc / l).T.astype(o_ref.dtype)
        if nxt is None:
            return None
        return jnp.broadcast_to(jnp.max(mrun, axis=0, keepdims=True), (sub, tq))

    bufs = (s0_ref, s1_ref)
    m = step((0, 0), bufs[0], None, None, None)
    for hd in range(heads):
        def run_steps(first, count, m, hd=hd):
            for u in range(count):
                m = step((hd, first + u + 1), bufs[(u + 1) % 2], (hd, first + u), bufs[u % 2], m)
            return m

        trips = (nq - 1) // unroll
        m = lax.fori_loop(0, trips, lambda j, m, run=run_steps: run(j * unroll, unroll, m), m)
        m = run_steps(trips * unroll, nq - 1 - trips * unroll, m)
        nxt = (hd + 1, 0) if hd + 1 < heads else None
        m = step(nxt, bufs[0], (hd, nq - 1), bufs[1], m)


def _attention(q, k, v, tq, tk, unroll, heads):
    b, s, _ = q.shape
    assert s % (2 * tq) == 0 and s % tk == 0 and tk % LANES == 0 and unroll % 2 == 0
    assert MLA_HEADS % heads == 0
    return pl.pallas_call(
        functools.partial(_attn_kernel, tq=tq, tk=tk, unroll=unroll, heads=heads),
        grid=(b, MLA_HEADS // heads),
        in_specs=[pl.BlockSpec((None, s, heads * QK_PAD), lambda bi, gi: (bi, 0, gi)),
                  pl.BlockSpec((None, s, heads * QK_PAD), lambda bi, gi: (bi, 0, gi)),
                  pl.BlockSpec((None, s, heads * V_HEAD), lambda bi, gi: (bi, 0, gi))],
        out_specs=pl.BlockSpec((None, s, heads * V_HEAD), lambda bi, gi: (bi, 0, gi)),
        out_shape=jax.ShapeDtypeStruct((b, s, MLA_HEADS * V_HEAD), BF16),
        scratch_shapes=[pltpu.VMEM((s, tq), F32), pltpu.VMEM((s, tq), F32),
                        pltpu.VMEM((heads, V_HEAD, s), BF16)],
        compiler_params=pltpu.CompilerParams(dimension_semantics=("arbitrary", "arbitrary"),
                                             vmem_limit_bytes=VMEM_LIMIT),
        name="attention",
    )(q, k, v)


def _hgrn_kernel(qf_ref, vf_ref, ff_ref, qb_ref, vb_ref, fb_ref, lb_ref, tri_ref,
                 of_ref, ob_ref, st_ref, *, n_chunks):
    @pl.when(pl.program_id(1) == 0)
    def _():
        st_ref[...] = jnp.zeros_like(st_ref)

    t_idx = lax.broadcasted_iota(jnp.int32, (CHUNK, CHUNK), 0)
    s_idx = lax.broadcasted_iota(jnp.int32, (CHUNK, CHUNK), 1)
    streams = (
        (qf_ref, vf_ref, ff_ref, None, of_ref, t_idx >= s_idx, CHUNK // 2 - 1, CHUNK - 1),
        (qb_ref, vb_ref, fb_ref, None, ob_ref, t_idx <= s_idx, CHUNK // 2, 0),
    )
    gate_half = [0.5 * (1.0 - lb_ref[d:d + 1, :]) for d in range(2)]
    gate_mid = [0.5 * (1.0 + lb_ref[d:d + 1, :]) for d in range(2)]

    nt = (((1,), (1,)), ((), ()))
    tn = (((0,), (0,)), ((), ()))
    heads = [slice(hd * HG_DK, (hd + 1) * HG_DK) for hd in range(HG_HEADS)]

    def chunk_of(d, i):
        return i if d == 0 else n_chunks - 1 - i

    cum, gated, intra = {}, {}, {}
    state = {(d, hd): st_ref[d, hd] for d in range(2) for hd in range(HG_HEADS)}

    def stage_cumsum(d, i):
        f_ref = streams[d][2]
        rows = slice(chunk_of(d, i) * CHUNK, (chunk_of(d, i) + 1) * CHUNK)
        ct = gate_half[d] * jnp.tanh(f_ref[rows, :])
        k = gate_half[d] - ct
        lf = jnp.log2(gate_mid[d] + ct)
        lf_hi = lf.astype(BF16)
        lf_lo = (lf - lf_hi.astype(F32)).astype(BF16)
        hi_lo = jnp.concatenate([lf_hi, lf_lo], axis=0)
        cum[d, i] = (jnp.dot(tri_ref[d], hi_lo, preferred_element_type=F32), k)

    def stage_gates(d, i):
        q_ref, _, _, _, _, _, mid, last = streams[d]
        rows = slice(chunk_of(d, i) * CHUNK, (chunk_of(d, i) + 1) * CHUNK)
        b, k = cum.pop((d, i))
        b_mid = b[mid:mid + 1, :]
        b_last = b[last:last + 1, :]
        e_mid = jnp.exp2(b - b_mid)
        q_in = q_ref[rows, :] * e_mid
        k_in = k / e_mid
        q_st = (q_in * jnp.exp2(b_mid)).astype(BF16)
        k_st = (k_in * jnp.exp2(b_last - b_mid)).astype(BF16)
        gated[d, i] = (q_in.astype(BF16), k_in.astype(BF16), q_st, k_st, jnp.exp2(b_last))

    def stage_intra(d, i):
        v_ref, mask = streams[d][1], streams[d][5]
        rows = slice(chunk_of(d, i) * CHUNK, (chunk_of(d, i) + 1) * CHUNK)
        q_in, k_in, q_st, k_st, decay = gated.pop((d, i))
        v = v_ref[rows, :]
        a, kv_t = [], []
        for sl in heads:
            a_h = lax.dot_general(q_in[:, sl], k_in[:, sl], nt, preferred_element_type=F32)
            a.append(jnp.where(mask, a_h, 0.0).astype(BF16))
            kv_t.append(lax.dot_general(v[:, sl], k_st[:, sl], tn, preferred_element_type=F32))
        intra[d, i] = (rows, v, q_st, a, kv_t, decay)

    def stage_state(d, i):
        o_ref = streams[d][4]
        rows, v, q_st, a, kv_t, decay = intra.pop((d, i))
        for hd, sl in enumerate(heads):
            st = state[d, hd]
            o = jnp.dot(a[hd], v[:, sl], preferred_element_type=F32)
            o += jnp.dot(q_st[:, sl], st.T.astype(BF16), preferred_element_type=F32)
            o_ref[rows, sl] = o
            state[d, hd] = st * decay[:, sl] + kv_t[hd]

    stages = (stage_cumsum, stage_gates, stage_intra, stage_state)
    for t in range(n_chunks + len(stages) - 1):
        for lag, stage in enumerate(stages):
            if 0 <= t - lag < n_chunks:
                for d in range(2):
                    stage(d, t - lag)
    for (d, hd), st in state.items():
        st_ref[d, hd] = st


def _hgrn(hq, hv, hff, hfb, lb, tri, blk):
    b, s, _ = hq.shape
    nb = s // blk
    fwd = pl.BlockSpec((None, blk, HG_W), lambda bi, j: (bi, j, 0))
    bwd = pl.BlockSpec((None, blk, HG_W), lambda bi, j: (bi, nb - 1 - j, 0))
    return pl.pallas_call(
        functools.partial(_hgrn_kernel, n_chunks=blk // CHUNK),
        grid=(b, nb),
        in_specs=[fwd] * 3 + [bwd] * 3 + [_const_spec(lb.shape), _const_spec(tri.shape)],
        out_specs=[fwd, bwd],
        out_shape=[jax.ShapeDtypeStruct((b, s, HG_W), F32)] * 2,
        scratch_shapes=[pltpu.VMEM((2, HG_HEADS, HG_DV, HG_DK), F32)],
        compiler_params=pltpu.CompilerParams(dimension_semantics=("arbitrary", "arbitrary"),
                                             vmem_limit_bytes=VMEM_LIMIT),
        name="hgrn2",
    )(hq, hv, hff, hq, hv, hfb, lb, tri)


def _out_kernel(x_ref, a_ref, of_ref, ob_ref, hg_ref, p_ref, ghgo_ref, woa_ref, wor_ref, gffn_ref,
                wg_ref, wu_ref, wd_ref, gple_ref, wpg_ref, wpp_ref, out_ref, *, ff_chunk):
    o = of_ref[...] + ob_ref[...]
    parts = []
    for hd in range(HG_HEADS):
        oh = o[:, hd * HG_DV:(hd + 1) * HG_DV]
        parts.append(oh * _rms_scale(jnp.sum(oh * oh, axis=-1, keepdims=True), HG_DV))
    r = jnp.concatenate(parts, axis=-1) * ghgo_ref[...] * hg_ref[...].astype(F32)
    x = (x_ref[...]
         + jnp.dot(a_ref[...], woa_ref[...], preferred_element_type=F32)
         + jnp.dot(r.astype(BF16), wor_ref[...], preferred_element_type=F32))

    h = (x * _rms_scale(jnp.sum(x * x, axis=-1, keepdims=True), D_MODEL) * gffn_ref[...]).astype(BF16)
    ffn = None
    for c in range(D_FF // ff_chunk):
        cs = slice(c * ff_chunk, (c + 1) * ff_chunk)
        gate = jnp.dot(h, wg_ref[:, cs], preferred_element_type=F32)
        up = jnp.dot(h, wu_ref[:, cs], preferred_element_type=F32)
        act = (_silu(gate) * up).astype(BF16)
        down = jnp.dot(act, wd_ref[cs, :], preferred_element_type=F32)
        ffn = down if ffn is None else ffn + down
    x = x + ffn

    h = (x * _rms_scale(jnp.sum(x * x, axis=-1, keepdims=True), D_MODEL) * gple_ref[...]).astype(BF16)
    gate = _sigmoid(jnp.dot(h, wpg_ref[...], preferred_element_type=F32))
    proj = jnp.dot(p_ref[...].astype(BF16), wpp_ref[...], preferred_element_type=F32)
    out_ref[...] = x + gate * proj


def _out_ffn_ple(x2, a, o_f, o_b, hg, p2, ghgo, woa, wor, gffn, wg, wu, wd, gple, wpg, wpp, tm):
    t = x2.shape[0]
    row = lambda w: pl.BlockSpec((tm, w), lambda i: (i, 0))
    consts = (ghgo, woa, wor, gffn, wg, wu, wd, gple, wpg, wpp)
    return pl.pallas_call(
        functools.partial(_out_kernel, ff_chunk=FF_CHUNK),
        grid=(t // tm,),
        in_specs=[row(D_MODEL), row(HG_W), row(HG_W), row(HG_W), row(HG_W), row(PLE_DIM)]
                 + [_const_spec(c.shape) for c in consts],
        out_specs=row(D_MODEL),
        out_shape=jax.ShapeDtypeStruct((t, D_MODEL), F32),
        compiler_params=pltpu.CompilerParams(dimension_semantics=("arbitrary",),
                                             vmem_limit_bytes=VMEM_LIMIT),
        name="out_ffn_ple",
    )(x2, a, o_f, o_b, hg, p2, *consts)


def _swap_halves(w):
    half = w.shape[-1] // 2
    return jnp.concatenate([w[..., half:], w[..., :half]], axis=-1)


def _pad_lanes(w):
    return jnp.pad(w, [(0, 0)] * (w.ndim - 1) + [(0, LANES - w.shape[-1])])


def kernel(x, p, positions, g_mix, w_in, g_qa, g_kva, w_qb, w_kvb, g_qn, g_kn, lb_param, g_hgo,
           w_o, g_ffn, w_gate, w_up, w_down, g_ple, w_ple_gate, w_ple_proj):
    bsz, seq, _ = x.shape
    t = bsz * seq
    layer = 0

    inv_freq = (ROPE_THETA ** (-np.arange(0, QK_ROPE, 2, dtype=np.float64) / QK_ROPE)).astype(np.float32)
    freq_row = jnp.asarray(np.tile(inv_freq, TOK_PER_ROW)[None, :])
    cos_t, sin_t = _rope_tables(positions.reshape(t // TOK_PER_ROW, TOK_PER_ROW), freq_row)

    wi = w_in[layer]
    c_q, c_kv, k_r, h5 = (wi[:, :Q_LORA], wi[:, Q_LORA:Q_LORA + KV_LORA],
                          wi[:, Q_LORA + KV_LORA:Q_LORA + KV_LORA + QK_ROPE],
                          wi[:, Q_LORA + KV_LORA + QK_ROPE:])
    half = np.repeat(np.array([1.0, 0.5, 0.5, 1.0, 1.0], np.float32), HG_W)[None, :]
    w1 = jnp.concatenate([c_q, c_kv, h5 * half, _pad_lanes(k_r), _pad_lanes(_swap_halves(k_r))],
                         axis=1).astype(BF16)

    wq = w_qb[layer].reshape(Q_LORA, MLA_HEADS, QK_HEAD)
    wq_n = wq[:, :, :QK_NOPE].reshape(Q_LORA, -1)
    wq_r = _pad_lanes(wq[:, :, QK_NOPE:]).reshape(Q_LORA, -1)
    wq_rs = _pad_lanes(_swap_halves(wq[:, :, QK_NOPE:])).reshape(Q_LORA, -1)
    wq2 = jnp.concatenate([wq_n, wq_r, wq_rs], axis=1).astype(BF16)

    wkv = w_kvb[layer].reshape(KV_LORA, MLA_HEADS, QK_NOPE + V_HEAD)
    wkv2 = jnp.concatenate([wkv[:, :, :QK_NOPE].reshape(KV_LORA, -1),
                            wkv[:, :, QK_NOPE:].reshape(KV_LORA, -1)], axis=1).astype(BF16)

    def qk_gain(g):
        g_r = g[QK_NOPE:]
        return jnp.stack([g[:QK_NOPE], _pad_lanes(g_r), _pad_lanes(_swap_halves(g_r))]).astype(F32)

    lower = jax.nn.softmax(lb_param.astype(F32), axis=0)[layer]
    row = lambda g: g.reshape(1, -1).astype(F32)

    x2 = x.reshape(t, D_MODEL)
    q, k, v, hq, hv, hff, hfb, hg = _in_proj(
        x2, cos_t, sin_t, row(g_mix[layer]), w1, row(g_qa[layer]), row(g_kva[layer]), wq2, wkv2,
        qk_gain(g_qn[layer]), qk_gain(g_kn[layer]), tm=IN_PROJ_ROWS)

    a = _attention(q.reshape(bsz, seq, -1), k.reshape(bsz, seq, -1), v.reshape(bsz, seq, -1),
                   tq=ATTN_Q_BLOCK, tk=ATTN_KEY_CHUNK, unroll=ATTN_UNROLL, heads=1)

    tt = np.arange(CHUNK)
    tri = np.stack([tt[:, None] >= tt[None, :], tt[:, None] <= tt[None, :]])
    tri = jnp.asarray(np.concatenate([tri, tri], axis=-1), dtype=BF16)
    r3 = lambda z: z.reshape(bsz, seq, HG_W)
    o_f, o_b = _hgrn(r3(hq), r3(hv), r3(hff), r3(hfb), lower, tri, blk=HGRN_BLOCK)

    wo = w_o[layer].astype(BF16)
    out = _out_ffn_ple(
        x2, a.reshape(t, -1), o_f.reshape(t, HG_W), o_b.reshape(t, HG_W), hg, p[layer].reshape(t, PLE_DIM),
        row(g_hgo[layer]), wo[:MLA_HEADS * V_HEAD], wo[MLA_HEADS * V_HEAD:], row(g_ffn[layer]),
        w_gate[layer].astype(BF16), w_up[layer].astype(BF16), w_down[layer].astype(BF16),
        row(g_ple[layer]), w_ple_gate[layer].astype(BF16), w_ple_proj[layer].astype(BF16),
        tm=OUT_ROWS)
    return out.reshape(bsz, seq, D_MODEL)
```

```python
import functools

import jax
import jax.numpy as jnp
import numpy as np
from jax import lax
from jax.experimental import pallas as pl
from jax.experimental.pallas import tpu as pltpu

D_MODEL = 1024
MLA_HEADS = 4
QK_NOPE = 128
QK_ROPE = 64
V_HEAD = 128
Q_LORA = 256
KV_LORA = 256
QK_HEAD = QK_NOPE + QK_ROPE
HG_HEADS = 4
HG_DK = 128
HG_DV = 128
HG_W = HG_HEADS * HG_DK
CHUNK = 64
D_FF = 2816
PLE_DIM = 256
ROPE_THETA = 10000.0
EPS = 1e-6

LANES = 128
SUBLANES = 8
MXU_TILE = 256
QK_PAD = MXU_TILE
VMEM_LIMIT = 56 * 1024 * 1024

IN_PROJ_ROWS = 512
OUT_ROWS = 512
FF_CHUNK = MXU_TILE
ATTN_Q_BLOCK = MXU_TILE
ATTN_KEY_CHUNK = 2048
ATTN_UNROLL = 6
HGRN_BLOCK = 1024

BF16 = jnp.bfloat16
F32 = jnp.float32


def _const_spec(shape):
    return pl.BlockSpec(shape, lambda *_: (0,) * len(shape), pipeline_mode=pl.Buffered(1))


def _sigmoid(x):
    return 0.5 + 0.5 * jnp.tanh(0.5 * x)


def _silu(x):
    h = 0.5 * x
    return h + h * jnp.tanh(h)


def _rms_scale(ss, n):
    return lax.rsqrt(ss * (1.0 / n) + EPS)


N_FREQ = QK_ROPE // 2
TOK_PER_ROW = LANES // N_FREQ


def _split3(x):
    p1 = x.astype(BF16)
    r1 = x - p1.astype(F32)
    p2 = r1.astype(BF16)
    p3 = (r1 - p2.astype(F32)).astype(BF16)
    return p1, p2, p3


def _rope_kernel(pos_ref, freq_ref, selc_ref, sels_ref, cos_ref, sin_ref):
    rows = pos_ref.shape[0]
    pos = pos_ref[...].astype(F32)
    lane = lax.broadcasted_iota(jnp.int32, (rows, LANES), 1)
    pos_d = jnp.broadcast_to(pos[:, TOK_PER_ROW - 1:TOK_PER_ROW], (rows, LANES))
    for i in range(TOK_PER_ROW - 2, -1, -1):
        pos_d = jnp.where(lane < (i + 1) * N_FREQ, jnp.broadcast_to(pos[:, i:i + 1], (rows, LANES)), pos_d)
    ang = pos_d * freq_ref[...]
    for trig, sel_ref, out_ref in ((jnp.cos(ang), selc_ref, cos_ref), (jnp.sin(ang), sels_ref, sin_ref)):
        pieces = jnp.concatenate(_split3(trig), axis=-1)
        for i in range(TOK_PER_ROW):
            out_ref[pl.ds(i, rows, stride=TOK_PER_ROW), :] = jnp.dot(
                pieces, sel_ref[i], preferred_element_type=F32)


def _rope_select_matrices():
    selc = np.zeros((TOK_PER_ROW, 3 * LANES, LANES), np.float32)
    sels = np.zeros_like(selc)
    j = np.arange(N_FREQ)
    for i in range(TOK_PER_ROW):
        for piece in range(3):
            for m in range(LANES // N_FREQ):
                selc[i, piece * LANES + N_FREQ * i + j, N_FREQ * m + j] = 1.0
                sels[i, piece * LANES + N_FREQ * i + j, N_FREQ * m + j] = 1.0 if m % 2 else -1.0
    return jnp.asarray(selc, BF16), jnp.asarray(sels, BF16)


def _rope_tables(pos4, freq_row):
    rows = pos4.shape[0]
    blk = 1024
    selc, sels = _rope_select_matrices()
    out = pl.BlockSpec((blk * TOK_PER_ROW, LANES), lambda i: (i, 0))
    return pl.pallas_call(
        _rope_kernel,
        grid=(rows // blk,),
        in_specs=[pl.BlockSpec((blk, TOK_PER_ROW), lambda i: (i, 0)), _const_spec(freq_row.shape),
                  _const_spec(selc.shape), _const_spec(sels.shape)],
        out_specs=[out, out],
        out_shape=[jax.ShapeDtypeStruct((rows * TOK_PER_ROW, LANES), F32)] * 2,
        compiler_params=pltpu.CompilerParams(dimension_semantics=("arbitrary",)),
        name="rope_tables",
    )(pos4, freq_row, selc, sels)


def _in_proj_kernel(x_ref, cos_ref, sin_ref, gmix_ref, wlat_ref, wkr_ref, whg_ref, gqa_ref, gkva_ref,
                    wq2_ref, wkv2_ref, gq_ref, gk_ref,
                    q_ref, k_ref, v_ref, hq_ref, hv_ref, hff_ref, hfb_ref, hg_ref):
    x = x_ref[...]
    h = (x * _rms_scale(jnp.sum(x * x, axis=-1, keepdims=True), D_MODEL) * gmix_ref[...]).astype(BF16)
    z_lat = jnp.dot(h, wlat_ref[...], preferred_element_type=F32)
    z_kr = jnp.dot(h, wkr_ref[...], preferred_element_type=F32)
    z_hg = jnp.dot(h, whg_ref[...], preferred_element_type=F32)

    c_q = z_lat[:, 0:Q_LORA]
    c_kv = z_lat[:, Q_LORA:Q_LORA + KV_LORA]
    hq = z_hg[:, 0:HG_W]
    hff = z_hg[:, HG_W:2 * HG_W]
    hfb = z_hg[:, 2 * HG_W:3 * HG_W]
    hi = z_hg[:, 3 * HG_W:4 * HG_W]
    hg = z_hg[:, 4 * HG_W:5 * HG_W]
    k_r = z_kr[:, 0:LANES]
    k_rs = z_kr[:, LANES:2 * LANES]

    cos = cos_ref[...]
    sin = sin_ref[...]

    cqn = c_q * _rms_scale(jnp.sum(c_q * c_q, axis=-1, keepdims=True), Q_LORA) * gqa_ref[...]
    q_all = jnp.dot(cqn.astype(BF16), wq2_ref[...], preferred_element_type=F32)
    gq_n, gq_r, gq_rs = gq_ref[0:1, :], gq_ref[1:2, :], gq_ref[2:3, :]
    scale = QK_HEAD ** -0.5 * np.log2(np.e)
    nh = MLA_HEADS
    for hd in range(nh):
        qn = q_all[:, hd * LANES:(hd + 1) * LANES]
        qr = q_all[:, (nh + hd) * LANES:(nh + hd + 1) * LANES]
        qrs = q_all[:, (2 * nh + hd) * LANES:(2 * nh + hd + 1) * LANES]
        ss = jnp.sum(qn * qn + qr * qr, axis=-1, keepdims=True)
        r = _rms_scale(ss, QK_HEAD) * scale
        q_ref[:, hd * QK_PAD:hd * QK_PAD + LANES] = (qn * gq_n * r).astype(BF16)
        q_ref[:, hd * QK_PAD + LANES:(hd + 1) * QK_PAD] = (
            (qr * gq_r * cos + qrs * gq_rs * sin) * r).astype(BF16)

    ckvn = c_kv * _rms_scale(jnp.sum(c_kv * c_kv, axis=-1, keepdims=True), KV_LORA) * gkva_ref[...]
    kv_all = jnp.dot(ckvn.astype(BF16), wkv2_ref[...], preferred_element_type=F32)
    gk_n, gk_r, gk_rs = gk_ref[0:1, :], gk_ref[1:2, :], gk_ref[2:3, :]
    k_rot = k_r * gk_r * cos + k_rs * gk_rs * sin
    ss_r = jnp.sum(k_r * k_r, axis=-1, keepdims=True)
    for hd in range(nh):
        kn = kv_all[:, hd * LANES:(hd + 1) * LANES]
        ss = jnp.sum(kn * kn, axis=-1, keepdims=True) + ss_r
        r = _rms_scale(ss, QK_HEAD)
        k_ref[:, hd * QK_PAD:hd * QK_PAD + LANES] = (kn * gk_n * r).astype(BF16)
        k_ref[:, hd * QK_PAD + LANES:(hd + 1) * QK_PAD] = (k_rot * r).astype(BF16)
    v_ref[...] = kv_all[:, nh * LANES:].astype(BF16)

    hq_ref[...] = _silu(hq)
    hv_ref[...] = hi.astype(BF16)
    hff_ref[...] = hff
    hfb_ref[...] = hfb
    hg_ref[...] = _silu(hg).astype(BF16)


def _in_proj(x2, cos_t, sin_t, *consts, tm):
    t = x2.shape[0]
    row = lambda w: pl.BlockSpec((tm, w), lambda i: (i, 0))
    outs = [(MLA_HEADS * QK_PAD, BF16), (MLA_HEADS * QK_PAD, BF16), (MLA_HEADS * V_HEAD, BF16),
            (HG_W, F32), (HG_W, BF16), (HG_W, F32), (HG_W, F32), (HG_W, BF16)]
    return pl.pallas_call(
        _in_proj_kernel,
        grid=(t // tm,),
        in_specs=[row(D_MODEL), row(LANES), row(LANES)] + [_const_spec(c.shape) for c in consts],
        out_specs=[row(w) for w, _ in outs],
        out_shape=[jax.ShapeDtypeStruct((t, w), dt) for w, dt in outs],
        compiler_params=pltpu.CompilerParams(dimension_semantics=("arbitrary",),
                                             vmem_limit_bytes=VMEM_LIMIT),
        name="in_proj",
    )(x2, cos_t, sin_t, *consts)


def _attn_kernel(q_ref, k_ref, v_ref, o_ref, s0_ref, s1_ref, vt_ref, *, tq, tk, unroll, heads):
    seq = k_ref.shape[0]
    nq, nk = seq // tq, seq // tk
    nt = (((1,), (1,)), ((), ()))
    sub = SUBLANES

    def fold(x, op):
        return op(x.reshape(tk // sub, sub, tq), axis=0)

    def q_rows(i):
        start = i * tq
        return pl.ds(start if isinstance(start, int) else pl.multiple_of(start, tq), tq)

    for hd in range(heads):
        vt_ref[hd] = v_ref[:, hd * V_HEAD:(hd + 1) * V_HEAD].astype(F32).T.astype(BF16)

    def step(nxt, s_w, cur, s_r, m_cur):
        if nxt is not None:
            qk_n = slice(nxt[0] * QK_PAD, (nxt[0] + 1) * QK_PAD)
            q = q_ref[q_rows(nxt[1]), qk_n]
            mrun = jnp.full((sub, tq), -jnp.inf, F32)
        if cur is not None:
            lrun = jnp.zeros((sub, tq), F32)
            acc = jnp.zeros((V_HEAD, tq), F32)
        for kc in range(nk):
            ks = slice(kc * tk, (kc + 1) * tk)
            if nxt is not None:
                s_c = lax.dot_general(k_ref[ks, qk_n], q, nt, preferred_element_type=F32)
                s_w[ks, :] = s_c
                mrun = jnp.maximum(mrun, fold(s_c, jnp.max))
            if cur is not None:
                p_c = jnp.exp2(s_r[ks, :].reshape(tk // sub, sub, tq) - m_cur[None]).reshape(tk, tq)
                lrun = lrun + fold(p_c, jnp.sum)
                acc = acc + jnp.dot(vt_ref[cur[0], :, ks], p_c.astype(BF16), preferred_element_type=F32)
        if cur is not None:
            l = jnp.sum(lrun, axis=0, keepdims=True)
            o_ref[q_rows(cur[1]), cur[0] * V_HEAD:(cur[0] + 1) * V_HEAD] = (acc / l).T.astype(o_ref.dtype)
        if nxt is None:
            return None
        return jnp.broadcast_to(jnp.max(mrun, axis=0, keepdims=True), (sub, tq))

    bufs = (s0_ref, s1_ref)
    m = step((0, 0), bufs[0], None, None, None)
    for hd in range(heads):
        def run_steps(first, count, m, hd=hd):
            for u in range(count):
                m = step((hd, first + u + 1), bufs[(u + 1) % 2], (hd, first + u), bufs[u % 2], m)
            return m

        trips = (nq - 1) // unroll
        m = lax.fori_loop(0, trips, lambda j, m, run=run_steps: run(j * unroll, unroll, m), m)
        m = run_steps(trips * unroll, nq - 1 - trips * unroll, m)
        nxt = (hd + 1, 0) if hd + 1 < heads else None
        m = step(nxt, bufs[0], (hd, nq - 1), bufs[1], m)


def _attention(q, k, v, tq, tk, unroll, heads):
    b, s, _ = q.shape
    assert s % (2 * tq) == 0 and s % tk == 0 and tk % LANES == 0 and unroll % 2 == 0
    assert MLA_HEADS % heads == 0
    return pl.pallas_call(
        functools.partial(_attn_kernel, tq=tq, tk=tk, unroll=unroll, heads=heads),
        grid=(b, MLA_HEADS // heads),
        in_specs=[pl.BlockSpec((None, s, heads * QK_PAD), lambda bi, gi: (bi, 0, gi)),
                  pl.BlockSpec((None, s, heads * QK_PAD), lambda bi, gi: (bi, 0, gi)),
                  pl.BlockSpec((None, s, heads * V_HEAD), lambda bi, gi: (bi, 0, gi))],
        out_specs=pl.BlockSpec((None, s, heads * V_HEAD), lambda bi, gi: (bi, 0, gi)),
        out_shape=jax.ShapeDtypeStruct((b, s, MLA_HEADS * V_HEAD), BF16),
        scratch_shapes=[pltpu.VMEM((s, tq), F32), pltpu.VMEM((s, tq), F32),
                        pltpu.VMEM((heads, V_HEAD, s), BF16)],
        compiler_params=pltpu.CompilerParams(dimension_semantics=("arbitrary", "arbitrary"),
                                             vmem_limit_bytes=VMEM_LIMIT),
        name="attention",
    )(q, k, v)


def _hgrn_kernel(qf_ref, vf_ref, ff_ref, qb_ref, vb_ref, fb_ref, lb_ref, tri_ref,
                 of_ref, ob_ref, st_ref, *, n_chunks):
    @pl.when(pl.program_id(1) == 0)
    def _():
        st_ref[...] = jnp.zeros_like(st_ref)

    t_idx = lax.broadcasted_iota(jnp.int32, (CHUNK, CHUNK), 0)
    s_idx = lax.broadcasted_iota(jnp.int32, (CHUNK, CHUNK), 1)
    streams = (
        (qf_ref, vf_ref, ff_ref, None, of_ref, t_idx >= s_idx, CHUNK // 2 - 1, CHUNK - 1),
        (qb_ref, vb_ref, fb_ref, None, ob_ref, t_idx <= s_idx, CHUNK // 2, 0),
    )
    gate_half = [0.5 * (1.0 - lb_ref[d:d + 1, :]) for d in range(2)]
    gate_mid = [0.5 * (1.0 + lb_ref[d:d + 1, :]) for d in range(2)]

    nt = (((1,), (1,)), ((), ()))
    tn = (((0,), (0,)), ((), ()))
    heads = [slice(hd * HG_DK, (hd + 1) * HG_DK) for hd in range(HG_HEADS)]

    def chunk_of(d, i):
        return i if d == 0 else n_chunks - 1 - i

    cum, gated, intra = {}, {}, {}
    state = {(d, hd): st_ref[d, hd] for d in range(2) for hd in range(HG_HEADS)}

    def stage_cumsum(d, i):
        f_ref = streams[d][2]
        rows = slice(chunk_of(d, i) * CHUNK, (chunk_of(d, i) + 1) * CHUNK)
        ct = gate_half[d] * jnp.tanh(0.5 * f_ref[rows, :])
        k = gate_half[d] - ct
        lf = jnp.log2(gate_mid[d] + ct)
        lf_hi = lf.astype(BF16)
        lf_lo = (lf - lf_hi.astype(F32)).astype(BF16)
        hi_lo = jnp.concatenate([lf_hi, lf_lo], axis=0)
        cum[d, i] = (jnp.dot(tri_ref[d], hi_lo, preferred_element_type=F32), k)

    def stage_gates(d, i):
        q_ref, _, _, _, _, _, mid, last = streams[d]
        rows = slice(chunk_of(d, i) * CHUNK, (chunk_of(d, i) + 1) * CHUNK)
        b, k = cum.pop((d, i))
        b_mid = b[mid:mid + 1, :]
        b_last = b[last:last + 1, :]
        e_mid = jnp.exp2(b - b_mid)
        q_in = q_ref[rows, :] * e_mid
        k_in = k / e_mid
        q_st = (q_in * jnp.exp2(b_mid)).astype(BF16)
        k_st = (k_in * jnp.exp2(b_last - b_mid)).astype(BF16)
        gated[d, i] = (q_in.astype(BF16), k_in.astype(BF16), q_st, k_st, jnp.exp2(b_last))

    def stage_intra(d, i):
        v_ref, mask = streams[d][1], streams[d][5]
        rows = slice(chunk_of(d, i) * CHUNK, (chunk_of(d, i) + 1) * CHUNK)
        q_in, k_in, q_st, k_st, decay = gated.pop((d, i))
        v = v_ref[rows, :]
        a, kv_t = [], []
        for sl in heads:
            a_h = lax.dot_general(q_in[:, sl], k_in[:, sl], nt, preferred_element_type=F32)
            a.append(jnp.where(mask, a_h, 0.0).astype(BF16))
            kv_t.append(lax.dot_general(v[:, sl], k_st[:, sl], tn, preferred_element_type=F32))
        intra[d, i] = (rows, v, q_st, a, kv_t, decay)

    def stage_state(d, i):
        o_ref = streams[d][4]
        rows, v, q_st, a, kv_t, decay = intra.pop((d, i))
        for hd, sl in enumerate(heads):
            st = state[d, hd]
            o = jnp.dot(a[hd], v[:, sl], preferred_element_type=F32)
            o += jnp.dot(q_st[:, sl], st.T.astype(BF16), preferred_element_type=F32)
            o_ref[rows, sl] = o
            state[d, hd] = st * decay[:, sl] + kv_t[hd]

    stages = (stage_cumsum, stage_gates, stage_intra, stage_state)
    for t in range(n_chunks + len(stages) - 1):
        for lag, stage in enumerate(stages):
            if 0 <= t - lag < n_chunks:
                for d in range(2):
                    stage(d, t - lag)
    for (d, hd), st in state.items():
        st_ref[d, hd] = st


def _hgrn(hq, hv, hff, hfb, lb, tri, blk):
    b, s, _ = hq.shape
    nb = s // blk
    fwd = pl.BlockSpec((None, blk, HG_W), lambda bi, j: (bi, j, 0))
    bwd = pl.BlockSpec((None, blk, HG_W), lambda bi, j: (bi, nb - 1 - j, 0))
    return pl.pallas_call(
        functools.partial(_hgrn_kernel, n_chunks=blk // CHUNK),
        grid=(b, nb),
        in_specs=[fwd] * 3 + [bwd] * 3 + [_const_spec(lb.shape), _const_spec(tri.shape)],
        out_specs=[fwd, bwd],
        out_shape=[jax.ShapeDtypeStruct((b, s, HG_W), F32)] * 2,
        scratch_shapes=[pltpu.VMEM((2, HG_HEADS, HG_DV, HG_DK), F32)],
        compiler_params=pltpu.CompilerParams(dimension_semantics=("arbitrary", "arbitrary"),
                                             vmem_limit_bytes=VMEM_LIMIT),
        name="hgrn2",
    )(hq, hv, hff, hq, hv, hfb, lb, tri)


def _out_kernel(x_ref, a_ref, of_ref, ob_ref, hg_ref, p_ref, ghgo_ref, wo_ref, gffn_ref,
                wg_ref, wu_ref, wd_ref, gple_ref, wpg_ref, wpp_ref, out_ref, *, ff_chunk):
    o = of_ref[...] + ob_ref[...]
    parts = []
    for hd in range(HG_HEADS):
        oh = o[:, hd * HG_DV:(hd + 1) * HG_DV]
        parts.append(oh * _rms_scale(jnp.sum(oh * oh, axis=-1, keepdims=True), HG_DV))
    r = jnp.concatenate(parts, axis=-1) * ghgo_ref[...] * hg_ref[...].astype(F32)
    x = (x_ref[...]
         + jnp.dot(a_ref[...], wo_ref[:MLA_HEADS * V_HEAD, :], preferred_element_type=F32)
         + jnp.dot(r.astype(BF16), wo_ref[MLA_HEADS * V_HEAD:, :], preferred_element_type=F32))

    h = (x * _rms_scale(jnp.sum(x * x, axis=-1, keepdims=True), D_MODEL) * gffn_ref[...]).astype(BF16)
    ffn = None
    for c in range(D_FF // ff_chunk):
        cs = slice(c * ff_chunk, (c + 1) * ff_chunk)
        gate = jnp.dot(h, wg_ref[:, cs], preferred_element_type=F32)
        up = jnp.dot(h, wu_ref[:, cs], preferred_element_type=F32)
        act = (_silu(gate) * up).astype(BF16)
        down = jnp.dot(act, wd_ref[cs, :], preferred_element_type=F32)
        ffn = down if ffn is None else ffn + down
    x = x + ffn

    h = (x * _rms_scale(jnp.sum(x * x, axis=-1, keepdims=True), D_MODEL) * gple_ref[...]).astype(BF16)
    gate = _sigmoid(jnp.dot(h, wpg_ref[...], preferred_element_type=F32))
    proj = jnp.dot(p_ref[...].astype(BF16), wpp_ref[...], preferred_element_type=F32)
    out_ref[...] = x + gate * proj


def _out_ffn_ple(x2, a, o_f, o_b, hg, p2, ghgo, wo, gffn, wg, wu, wd, gple, wpg, wpp, tm):
    t = x2.shape[0]
    row = lambda w: pl.BlockSpec((tm, w), lambda i: (i, 0))
    consts = (ghgo, wo, gffn, wg, wu, wd, gple, wpg, wpp)
    return pl.pallas_call(
        functools.partial(_out_kernel, ff_chunk=FF_CHUNK),
        grid=(t // tm,),
        in_specs=[row(D_MODEL), row(HG_W), row(HG_W), row(HG_W), row(HG_W), row(PLE_DIM)]
                 + [_const_spec(c.shape) for c in consts],
        out_specs=row(D_MODEL),
        out_shape=jax.ShapeDtypeStruct((t, D_MODEL), F32),
        compiler_params=pltpu.CompilerParams(dimension_semantics=("arbitrary",),
                                             vmem_limit_bytes=VMEM_LIMIT),
        name="out_ffn_ple",
    )(x2, a, o_f, o_b, hg, p2, *consts)


def _swap_halves(w):
    half = w.shape[-1] // 2
    return jnp.concatenate([w[..., half:], w[..., :half]], axis=-1)


def _pad_lanes(w):
    return jnp.pad(w, [(0, 0)] * (w.ndim - 1) + [(0, LANES - w.shape[-1])])


def kernel(x, p, positions, g_mix, w_in, g_qa, g_kva, w_qb, w_kvb, g_qn, g_kn, lb_param, g_hgo,
           w_o, g_ffn, w_gate, w_up, w_down, g_ple, w_ple_gate, w_ple_proj):
    bsz, seq, _ = x.shape
    t = bsz * seq
    layer = 0

    inv_freq = (ROPE_THETA ** (-np.arange(0, QK_ROPE, 2, dtype=np.float64) / QK_ROPE)).astype(np.float32)
    freq_row = jnp.asarray(np.tile(inv_freq, TOK_PER_ROW)[None, :])
    cos_t, sin_t = _rope_tables(positions.reshape(t // TOK_PER_ROW, TOK_PER_ROW), freq_row)

    wi = w_in[layer]
    lat = Q_LORA + KV_LORA
    k_r = wi[:, lat:lat + QK_ROPE]
    w_lat = wi[:, :lat].astype(BF16)
    w_kr = jnp.concatenate([_pad_lanes(k_r), _pad_lanes(_swap_halves(k_r))], axis=1).astype(BF16)
    w_hg = wi[:, lat + QK_ROPE:].astype(BF16)

    wq = w_qb[layer].reshape(Q_LORA, MLA_HEADS, QK_HEAD)
    wq_n = wq[:, :, :QK_NOPE].reshape(Q_LORA, -1)
    wq_r = _pad_lanes(wq[:, :, QK_NOPE:]).reshape(Q_LORA, -1)
    wq_rs = _pad_lanes(_swap_halves(wq[:, :, QK_NOPE:])).reshape(Q_LORA, -1)
    wq2 = jnp.concatenate([wq_n, wq_r, wq_rs], axis=1).astype(BF16)

    wkv = w_kvb[layer].reshape(KV_LORA, MLA_HEADS, QK_NOPE + V_HEAD)
    wkv2 = jnp.concatenate([wkv[:, :, :QK_NOPE].reshape(KV_LORA, -1),
                            wkv[:, :, QK_NOPE:].reshape(KV_LORA, -1)], axis=1).astype(BF16)

    def qk_gain(g):
        g_r = g[QK_NOPE:]
        return jnp.stack([g[:QK_NOPE], _pad_lanes(g_r), _pad_lanes(_swap_halves(g_r))]).astype(F32)

    lower = jax.nn.softmax(lb_param.astype(F32), axis=0)[layer]
    row = lambda g: g.reshape(1, -1).astype(F32)

    x2 = x.reshape(t, D_MODEL)
    q, k, v, hq, hv, hff, hfb, hg = _in_proj(
        x2, cos_t, sin_t, row(g_mix[layer]), w_lat, w_kr, w_hg, row(g_qa[layer]), row(g_kva[layer]), wq2, wkv2,
        qk_gain(g_qn[layer]), qk_gain(g_kn[layer]), tm=IN_PROJ_ROWS)

    a = _attention(q.reshape(bsz, seq, -1), k.reshape(bsz, seq, -1), v.reshape(bsz, seq, -1),
                   tq=ATTN_Q_BLOCK, tk=ATTN_KEY_CHUNK, unroll=ATTN_UNROLL, heads=1)

    tt = np.arange(CHUNK)
    tri = np.stack([tt[:, None] >= tt[None, :], tt[:, None] <= tt[None, :]])
    tri = jnp.asarray(np.concatenate([tri, tri], axis=-1), dtype=BF16)
    r3 = lambda z: z.reshape(bsz, seq, HG_W)
    o_f, o_b = _hgrn(r3(hq), r3(hv), r3(hff), r3(hfb), lower, tri, blk=HGRN_BLOCK)

    out = _out_ffn_ple(
        x2, a.reshape(t, -1), o_f.reshape(t, HG_W), o_b.reshape(t, HG_W), hg, p[layer].reshape(t, PLE_DIM),
        row(g_hgo[layer]), w_o[layer].astype(BF16), row(g_ffn[layer]),
        w_gate[layer].astype(BF16), w_up[layer].astype(BF16), w_down[layer].astype(BF16),
        row(g_ple[layer]), w_ple_gate[layer].astype(BF16), w_ple_proj[layer].astype(BF16),
        tm=OUT_ROWS)
    return out.reshape(bsz, seq, D_MODEL)
```

```python
import functools

import jax
import jax.numpy as jnp
import numpy as np
from jax import lax
from jax.experimental import pallas as pl
from jax.experimental.pallas import tpu as pltpu

D_MODEL = 1024
MLA_HEADS = 4
QK_NOPE = 128
QK_ROPE = 64
V_HEAD = 128
Q_LORA = 256
KV_LORA = 256
QK_HEAD = QK_NOPE + QK_ROPE
HG_HEADS = 4
HG_DK = 128
HG_DV = 128
HG_W = HG_HEADS * HG_DK
CHUNK = 64
D_FF = 2816
PLE_DIM = 256
ROPE_THETA = 10000.0
EPS = 1e-6

LANES = 128
SUBLANES = 8
MXU_TILE = 256
QK_PAD = MXU_TILE
VMEM_LIMIT = 56 * 1024 * 1024

IN_PROJ_ROWS = 512
OUT_ROWS = 512
FF_CHUNK = MXU_TILE
ATTN_Q_BLOCK = MXU_TILE
ATTN_KEY_CHUNK = 2048
ATTN_UNROLL = 6
HGRN_BLOCK = 1024

BF16 = jnp.bfloat16
F32 = jnp.float32


def _const_spec(shape):
    return pl.BlockSpec(shape, lambda *_: (0,) * len(shape), pipeline_mode=pl.Buffered(1))


def _sigmoid(x):
    return 0.5 + 0.5 * jnp.tanh(0.5 * x)


def _silu(x):
    h = 0.5 * x
    return h + h * jnp.tanh(h)


def _rms_scale(ss, n):
    return lax.rsqrt(ss * (1.0 / n) + EPS)


N_FREQ = QK_ROPE // 2
TOK_PER_ROW = LANES // N_FREQ


def _split3(x):
    p1 = x.astype(BF16)
    r1 = x - p1.astype(F32)
    p2 = r1.astype(BF16)
    p3 = (r1 - p2.astype(F32)).astype(BF16)
    return p1, p2, p3


def _rope_kernel(pos_ref, freq_ref, selc_ref, sels_ref, cos_ref, sin_ref):
    rows = pos_ref.shape[0]
    pos = pos_ref[...].astype(F32)
    lane = lax.broadcasted_iota(jnp.int32, (rows, LANES), 1)
    pos_d = jnp.broadcast_to(pos[:, TOK_PER_ROW - 1:TOK_PER_ROW], (rows, LANES))
    for i in range(TOK_PER_ROW - 2, -1, -1):
        pos_d = jnp.where(lane < (i + 1) * N_FREQ, jnp.broadcast_to(pos[:, i:i + 1], (rows, LANES)), pos_d)
    ang = pos_d * freq_ref[...]
    for trig, sel_ref, out_ref in ((jnp.cos(ang), selc_ref, cos_ref), (jnp.sin(ang), sels_ref, sin_ref)):
        pieces = jnp.concatenate(_split3(trig), axis=-1)
        for i in range(TOK_PER_ROW):
            out_ref[pl.ds(i, rows, stride=TOK_PER_ROW), :] = jnp.dot(
                pieces, sel_ref[i], preferred_element_type=F32)


def _rope_select_matrices():
    selc = np.zeros((TOK_PER_ROW, 3 * LANES, LANES), np.float32)
    sels = np.zeros_like(selc)
    j = np.arange(N_FREQ)
    for i in range(TOK_PER_ROW):
        for piece in range(3):
            for m in range(LANES // N_FREQ):
                selc[i, piece * LANES + N_FREQ * i + j, N_FREQ * m + j] = 1.0
                sels[i, piece * LANES + N_FREQ * i + j, N_FREQ * m + j] = 1.0 if m % 2 else -1.0
    return jnp.asarray(selc, BF16), jnp.asarray(sels, BF16)


def _rope_tables(pos4, freq_row):
    rows = pos4.shape[0]
    blk = 1024
    selc, sels = _rope_select_matrices()
    out = pl.BlockSpec((blk * TOK_PER_ROW, LANES), lambda i: (i, 0))
    return pl.pallas_call(
        _rope_kernel,
        grid=(rows // blk,),
        in_specs=[pl.BlockSpec((blk, TOK_PER_ROW), lambda i: (i, 0)), _const_spec(freq_row.shape),
                  _const_spec(selc.shape), _const_spec(sels.shape)],
        out_specs=[out, out],
        out_shape=[jax.ShapeDtypeStruct((rows * TOK_PER_ROW, LANES), F32)] * 2,
        compiler_params=pltpu.CompilerParams(dimension_semantics=("arbitrary",)),
        name="rope_tables",
    )(pos4, freq_row, selc, sels)


def _in_proj_kernel(x_ref, cos_ref, sin_ref, gmix_ref, wlat_ref, wkr_ref, whg_ref, gqa_ref, gkva_ref,
                    wq2_ref, wkv2_ref, gq_ref, gk_ref,
                    q_ref, k_ref, v_ref, hq_ref, hv_ref, hff_ref, hfb_ref, hg_ref):
    x = x_ref[...]
    h = (x * _rms_scale(jnp.sum(x * x, axis=-1, keepdims=True), D_MODEL) * gmix_ref[...]).astype(BF16)
    nt = (((1,), (1,)), ((), ()))
    z_lat = lax.dot_general(h, wlat_ref[...], nt, preferred_element_type=F32)
    z_kr = lax.dot_general(h, wkr_ref[...], nt, preferred_element_type=F32)
    z_hg = lax.dot_general(h, whg_ref[...], nt, preferred_element_type=F32)

    c_q = z_lat[:, 0:Q_LORA]
    c_kv = z_lat[:, Q_LORA:Q_LORA + KV_LORA]
    hq = z_hg[:, 0:HG_W]
    hff = z_hg[:, HG_W:2 * HG_W]
    hfb = z_hg[:, 2 * HG_W:3 * HG_W]
    hi = z_hg[:, 3 * HG_W:4 * HG_W]
    hg = z_hg[:, 4 * HG_W:5 * HG_W]
    k_r = z_kr[:, 0:LANES]
    k_rs = z_kr[:, LANES:2 * LANES]

    cos = cos_ref[...]
    sin = sin_ref[...]

    cqn = c_q * _rms_scale(jnp.sum(c_q * c_q, axis=-1, keepdims=True), Q_LORA) * gqa_ref[...]
    q_all = jnp.dot(cqn.astype(BF16), wq2_ref[...], preferred_element_type=F32)
    gq_n, gq_r, gq_rs = gq_ref[0:1, :], gq_ref[1:2, :], gq_ref[2:3, :]
    scale = QK_HEAD ** -0.5 * np.log2(np.e)
    nh = MLA_HEADS
    for hd in range(nh):
        qn = q_all[:, hd * LANES:(hd + 1) * LANES]
        qr = q_all[:, (nh + hd) * LANES:(nh + hd + 1) * LANES]
        qrs = q_all[:, (2 * nh + hd) * LANES:(2 * nh + hd + 1) * LANES]
        ss = jnp.sum(qn * qn + qr * qr, axis=-1, keepdims=True)
        r = _rms_scale(ss, QK_HEAD) * scale
        q_ref[:, hd * QK_PAD:hd * QK_PAD + LANES] = (qn * gq_n * r).astype(BF16)
        q_ref[:, hd * QK_PAD + LANES:(hd + 1) * QK_PAD] = (
            (qr * gq_r * cos + qrs * gq_rs * sin) * r).astype(BF16)

    ckvn = c_kv * _rms_scale(jnp.sum(c_kv * c_kv, axis=-1, keepdims=True), KV_LORA) * gkva_ref[...]
    kv_all = jnp.dot(ckvn.astype(BF16), wkv2_ref[...], preferred_element_type=F32)
    gk_n, gk_r, gk_rs = gk_ref[0:1, :], gk_ref[1:2, :], gk_ref[2:3, :]
    k_rot = k_r * gk_r * cos + k_rs * gk_rs * sin
    ss_r = jnp.sum(k_r * k_r, axis=-1, keepdims=True)
    for hd in range(nh):
        kn = kv_all[:, hd * LANES:(hd + 1) * LANES]
        ss = jnp.sum(kn * kn, axis=-1, keepdims=True) + ss_r
        r = _rms_scale(ss, QK_HEAD)
        k_ref[:, hd * QK_PAD:hd * QK_PAD + LANES] = (kn * gk_n * r).astype(BF16)
        k_ref[:, hd * QK_PAD + LANES:(hd + 1) * QK_PAD] = (k_rot * r).astype(BF16)
    v_ref[...] = kv_all[:, nh * LANES:].astype(BF16)

    hq_ref[...] = _silu(hq)
    hv_ref[...] = hi.astype(BF16)
    hff_ref[...] = hff
    hfb_ref[...] = hfb
    hg_ref[...] = _silu(hg).astype(BF16)


def _in_proj(x2, cos_t, sin_t, *consts, tm):
    t = x2.shape[0]
    row = lambda w: pl.BlockSpec((tm, w), lambda i: (i, 0))
    outs = [(MLA_HEADS * QK_PAD, BF16), (MLA_HEADS * QK_PAD, BF16), (MLA_HEADS * V_HEAD, BF16),
            (HG_W, F32), (HG_W, BF16), (HG_W, F32), (HG_W, F32), (HG_W, BF16)]
    return pl.pallas_call(
        _in_proj_kernel,
        grid=(t // tm,),
        in_specs=[row(D_MODEL), row(LANES), row(LANES)] + [_const_spec(c.shape) for c in consts],
        out_specs=[row(w) for w, _ in outs],
        out_shape=[jax.ShapeDtypeStruct((t, w), dt) for w, dt in outs],
        compiler_params=pltpu.CompilerParams(dimension_semantics=("arbitrary",),
                                             vmem_limit_bytes=VMEM_LIMIT),
        name="in_proj",
    )(x2, cos_t, sin_t, *consts)


def _attn_kernel(q_ref, k_ref, v_ref, o_ref, s0_ref, s1_ref, vt_ref, *, tq, tk, unroll, heads):
    seq = k_ref.shape[0]
    nq, nk = seq // tq, seq // tk
    nt = (((1,), (1,)), ((), ()))
    sub = SUBLANES

    def fold(x, op):
        return op(x.reshape(tk // sub, sub, tq), axis=0)

    def q_rows(i):
        start = i * tq
        return pl.ds(start if isinstance(start, int) else pl.multiple_of(start, tq), tq)

    for hd in range(heads):
        vt_ref[hd] = v_ref[:, hd * V_HEAD:(hd + 1) * V_HEAD].astype(F32).T.astype(BF16)

    def step(nxt, s_w, cur, s_r, m_cur):
        if nxt is not None:
            qk_n = slice(nxt[0] * QK_PAD, (nxt[0] + 1) * QK_PAD)
            q = q_ref[q_rows(nxt[1]), qk_n]
            mrun = jnp.full((sub, tq), -jnp.inf, F32)
        if cur is not None:
            lrun = jnp.zeros((sub, tq), F32)
            acc = jnp.zeros((V_HEAD, tq), F32)
        for kc in range(nk):
            ks = slice(kc * tk, (kc + 1) * tk)
            if nxt is not None:
                s_c = lax.dot_general(k_ref[ks, qk_n], q, nt, preferred_element_type=F32)
                s_w[ks, :] = s_c
                mrun = jnp.maximum(mrun, fold(s_c, jnp.max))
            if cur is not None:
                p_c = jnp.exp2(s_r[ks, :].reshape(tk // sub, sub, tq) - m_cur[None]).reshape(tk, tq)
                lrun = lrun + fold(p_c, jnp.sum)
                acc = acc + jnp.dot(vt_ref[cur[0], :, ks], p_c.astype(BF16), preferred_element_type=F32)
        if cur is not None:
            l = jnp.sum(lrun, axis=0, keepdims=True)
            o_ref[q_rows(cur[1]), cur[0] * V_HEAD:(cur[0] + 1) * V_HEAD] = (acc / l).T.astype(o_ref.dtype)
        if nxt is None:
            return None
        return jnp.broadcast_to(jnp.max(mrun, axis=0, keepdims=True), (sub, tq))

    bufs = (s0_ref, s1_ref)
    m = step((0, 0), bufs[0], None, None, None)
    for hd in range(heads):
        def run_steps(first, count, m, hd=hd):
            for u in range(count):
                m = step((hd, first + u + 1), bufs[(u + 1) % 2], (hd, first + u), bufs[u % 2], m)
            return m

        trips = (nq - 1) // unroll
        m = lax.fori_loop(0, trips, lambda j, m, run=run_steps: run(j * unroll, unroll, m), m)
        m = run_steps(trips * unroll, nq - 1 - trips * unroll, m)
        nxt = (hd + 1, 0) if hd + 1 < heads else None
        m = step(nxt, bufs[0], (hd, nq - 1), bufs[1], m)


def _attention(q, k, v, tq, tk, unroll, heads):
    b, s, _ = q.shape
    assert s % (2 * tq) == 0 and s % tk == 0 and tk % LANES == 0 and unroll % 2 == 0
    assert MLA_HEADS % heads == 0
    return pl.pallas_call(
        functools.partial(_attn_kernel, tq=tq, tk=tk, unroll=unroll, heads=heads),
        grid=(b, MLA_HEADS // heads),
        in_specs=[pl.BlockSpec((None, s, heads * QK_PAD), lambda bi, gi: (bi, 0, gi)),
                  pl.BlockSpec((None, s, heads * QK_PAD), lambda bi, gi: (bi, 0, gi)),
                  pl.BlockSpec((None, s, heads * V_HEAD), lambda bi, gi: (bi, 0, gi))],
        out_specs=pl.BlockSpec((None, s, heads * V_HEAD), lambda bi, gi: (bi, 0, gi)),
        out_shape=jax.ShapeDtypeStruct((b, s, MLA_HEADS * V_HEAD), BF16),
        scratch_shapes=[pltpu.VMEM((s, tq), F32), pltpu.VMEM((s, tq), F32),
                        pltpu.VMEM((heads, V_HEAD, s), BF16)],
        compiler_params=pltpu.CompilerParams(dimension_semantics=("arbitrary", "arbitrary"),
                                             vmem_limit_bytes=VMEM_LIMIT),
        name="attention",
    )(q, k, v)


def _hgrn_kernel(qf_ref, vf_ref, ff_ref, qb_ref, vb_ref, fb_ref, lb_ref, tri_ref,
                 of_ref, ob_ref, st_ref, *, n_chunks):
    @pl.when(pl.program_id(1) == 0)
    def _():
        st_ref[...] = jnp.zeros_like(st_ref)

    t_idx = lax.broadcasted_iota(jnp.int32, (CHUNK, CHUNK), 0)
    s_idx = lax.broadcasted_iota(jnp.int32, (CHUNK, CHUNK), 1)
    streams = (
        (qf_ref, vf_ref, ff_ref, None, of_ref, t_idx >= s_idx, CHUNK // 2 - 1, CHUNK - 1),
        (qb_ref, vb_ref, fb_ref, None, ob_ref, t_idx <= s_idx, CHUNK // 2, 0),
    )
    gate_half = [0.5 * (1.0 - lb_ref[d:d + 1, :]) for d in range(2)]
    gate_mid = [0.5 * (1.0 + lb_ref[d:d + 1, :]) for d in range(2)]

    nt = (((1,), (1,)), ((), ()))
    tn = (((0,), (0,)), ((), ()))
    heads = [slice(hd * HG_DK, (hd + 1) * HG_DK) for hd in range(HG_HEADS)]

    def chunk_of(d, i):
        return i if d == 0 else n_chunks - 1 - i

    cum, gated, intra = {}, {}, {}
    state = {(d, hd): st_ref[d, hd] for d in range(2) for hd in range(HG_HEADS)}

    def stage_cumsum(d, i):
        f_ref = streams[d][2]
        rows = slice(chunk_of(d, i) * CHUNK, (chunk_of(d, i) + 1) * CHUNK)
        ct = gate_half[d] * jnp.tanh(0.5 * f_ref[rows, :])
        k = gate_half[d] - ct
        lf = jnp.log2(gate_mid[d] + ct)
        lf_hi = lf.astype(BF16)
        lf_lo = (lf - lf_hi.astype(F32)).astype(BF16)
        hi_lo = jnp.concatenate([lf_hi, lf_lo], axis=0)
        cum[d, i] = (jnp.dot(tri_ref[d], hi_lo, preferred_element_type=F32), k)

    def stage_gates(d, i):
        q_ref, _, _, _, _, _, mid, last = streams[d]
        rows = slice(chunk_of(d, i) * CHUNK, (chunk_of(d, i) + 1) * CHUNK)
        b, k = cum.pop((d, i))
        b_mid = b[mid:mid + 1, :]
        b_last = b[last:last + 1, :]
        e_mid = jnp.exp2(b - b_mid)
        q_in = q_ref[rows, :] * e_mid
        k_in = k / e_mid
        q_st = (q_in * jnp.exp2(b_mid)).astype(BF16)
        k_st = (k_in * jnp.exp2(b_last - b_mid)).astype(BF16)
        gated[d, i] = (q_in.astype(BF16), k_in.astype(BF16), q_st, k_st, jnp.exp2(b_last))

    def stage_intra(d, i):
        v_ref, mask = streams[d][1], streams[d][5]
        rows = slice(chunk_of(d, i) * CHUNK, (chunk_of(d, i) + 1) * CHUNK)
        q_in, k_in, q_st, k_st, decay = gated.pop((d, i))
        v = v_ref[rows, :]
        a, kv_t = [], []
        for sl in heads:
            a_h = lax.dot_general(q_in[:, sl], k_in[:, sl], nt, preferred_element_type=F32)
            a.append(jnp.where(mask, a_h, 0.0).astype(BF16))
            kv_t.append(lax.dot_general(v[:, sl], k_st[:, sl], tn, preferred_element_type=F32))
        intra[d, i] = (rows, v, q_st, a, kv_t, decay)

    def stage_state(d, i):
        o_ref = streams[d][4]
        rows, v, q_st, a, kv_t, decay = intra.pop((d, i))
        for hd, sl in enumerate(heads):
            st = state[d, hd]
            o = jnp.dot(a[hd], v[:, sl], preferred_element_type=F32)
            o += jnp.dot(q_st[:, sl], st.T.astype(BF16), preferred_element_type=F32)
            o_ref[rows, sl] = o
            state[d, hd] = st * decay[:, sl] + kv_t[hd]

    stages = (stage_cumsum, stage_gates, stage_intra, stage_state)
    for t in range(n_chunks + len(stages) - 1):
        for lag, stage in enumerate(stages):
            if 0 <= t - lag < n_chunks:
                for d in range(2):
                    stage(d, t - lag)
    for (d, hd), st in state.items():
        st_ref[d, hd] = st


def _hgrn(hq, hv, hff, hfb, lb, tri, blk):
    b, s, _ = hq.shape
    nb = s // blk
    fwd = pl.BlockSpec((None, blk, HG_W), lambda bi, j: (bi, j, 0))
    bwd = pl.BlockSpec((None, blk, HG_W), lambda bi, j: (bi, nb - 1 - j, 0))
    return pl.pallas_call(
        functools.partial(_hgrn_kernel, n_chunks=blk // CHUNK),
        grid=(b, nb),
        in_specs=[fwd] * 3 + [bwd] * 3 + [_const_spec(lb.shape), _const_spec(tri.shape)],
        out_specs=[fwd, bwd],
        out_shape=[jax.ShapeDtypeStruct((b, s, HG_W), F32)] * 2,
        scratch_shapes=[pltpu.VMEM((2, HG_HEADS, HG_DV, HG_DK), F32)],
        compiler_params=pltpu.CompilerParams(dimension_semantics=("arbitrary", "arbitrary"),
                                             vmem_limit_bytes=VMEM_LIMIT),
        name="hgrn2",
    )(hq, hv, hff, hq, hv, hfb, lb, tri)


def _out_kernel(x_ref, a_ref, of_ref, ob_ref, hg_ref, p_ref, ghgo_ref, wo_ref, gffn_ref,
                wg_ref, wu_ref, wd_ref, gple_ref, wpg_ref, wpp_ref, out_ref, *, ff_chunk):
    o = of_ref[...] + ob_ref[...]
    parts = []
    for hd in range(HG_HEADS):
        oh = o[:, hd * HG_DV:(hd + 1) * HG_DV]
        parts.append(oh * _rms_scale(jnp.sum(oh * oh, axis=-1, keepdims=True), HG_DV))
    r = jnp.concatenate(parts, axis=-1) * ghgo_ref[...] * hg_ref[...].astype(F32)
    x = (x_ref[...]
         + jnp.dot(a_ref[...], wo_ref[:MLA_HEADS * V_HEAD, :], preferred_element_type=F32)
         + jnp.dot(r.astype(BF16), wo_ref[MLA_HEADS * V_HEAD:, :], preferred_element_type=F32))

    h = (x * _rms_scale(jnp.sum(x * x, axis=-1, keepdims=True), D_MODEL) * gffn_ref[...]).astype(BF16)
    ffn = None
    for c in range(D_FF // ff_chunk):
        cs = slice(c * ff_chunk, (c + 1) * ff_chunk)
        gate = jnp.dot(h, wg_ref[:, cs], preferred_element_type=F32)
        up = jnp.dot(h, wu_ref[:, cs], preferred_element_type=F32)
        act = (_silu(gate) * up).astype(BF16)
        down = jnp.dot(act, wd_ref[cs, :], preferred_element_type=F32)
        ffn = down if ffn is None else ffn + down
    x = x + ffn

    h = (x * _rms_scale(jnp.sum(x * x, axis=-1, keepdims=True), D_MODEL) * gple_ref[...]).astype(BF16)
    gate = _sigmoid(jnp.dot(h, wpg_ref[...], preferred_element_type=F32))
    proj = jnp.dot(p_ref[...].astype(BF16), wpp_ref[...], preferred_element_type=F32)
    out_ref[...] = x + gate * proj


def _out_ffn_ple(x2, a, o_f, o_b, hg, p2, ghgo, wo, gffn, wg, wu, wd, gple, wpg, wpp, tm):
    t = x2.shape[0]
    row = lambda w: pl.BlockSpec((tm, w), lambda i: (i, 0))
    consts = (ghgo, wo, gffn, wg, wu, wd, gple, wpg, wpp)
    return pl.pallas_call(
        functools.partial(_out_kernel, ff_chunk=FF_CHUNK),
        grid=(t // tm,),
        in_specs=[row(D_MODEL), row(HG_W), row(HG_W), row(HG_W), row(HG_W), row(PLE_DIM)]
                 + [_const_spec(c.shape) for c in consts],
        out_specs=row(D_MODEL),
        out_shape=jax.ShapeDtypeStruct((t, D_MODEL), F32),
        compiler_params=pltpu.CompilerParams(dimension_semantics=("arbitrary",),
                                             vmem_limit_bytes=VMEM_LIMIT),
        name="out_ffn_ple",
    )(x2, a, o_f, o_b, hg, p2, *consts)


def _swap_halves(w):
    half = w.shape[-1] // 2
    return jnp.concatenate([w[..., half:], w[..., :half]], axis=-1)


def _pad_lanes(w):
    return jnp.pad(w, [(0, 0)] * (w.ndim - 1) + [(0, LANES - w.shape[-1])])


def kernel(x, p, positions, g_mix, w_in, g_qa, g_kva, w_qb, w_kvb, g_qn, g_kn, lb_param, g_hgo,
           w_o, g_ffn, w_gate, w_up, w_down, g_ple, w_ple_gate, w_ple_proj):
    bsz, seq, _ = x.shape
    t = bsz * seq
    layer = 0

    inv_freq = (ROPE_THETA ** (-np.arange(0, QK_ROPE, 2, dtype=np.float64) / QK_ROPE)).astype(np.float32)
    freq_row = jnp.asarray(np.tile(inv_freq, TOK_PER_ROW)[None, :])
    cos_t, sin_t = _rope_tables(positions.reshape(t // TOK_PER_ROW, TOK_PER_ROW), freq_row)

    wi_t = jnp.swapaxes(w_in[layer], 0, 1)
    lat = Q_LORA + KV_LORA
    k_r = wi_t[lat:lat + QK_ROPE]
    gap = jnp.zeros((LANES - QK_ROPE, D_MODEL), wi_t.dtype)
    w_lat = wi_t[:lat].astype(BF16)
    k_rs = jnp.concatenate([k_r[QK_ROPE // 2:], k_r[:QK_ROPE // 2]], axis=0)
    w_kr = jnp.concatenate([k_r, gap, k_rs, gap], axis=0).astype(BF16)
    w_hg = wi_t[lat + QK_ROPE:].astype(BF16)

    wq = w_qb[layer].reshape(Q_LORA, MLA_HEADS, QK_HEAD)
    wq_n = wq[:, :, :QK_NOPE].reshape(Q_LORA, -1)
    wq_r = _pad_lanes(wq[:, :, QK_NOPE:]).reshape(Q_LORA, -1)
    wq_rs = _pad_lanes(_swap_halves(wq[:, :, QK_NOPE:])).reshape(Q_LORA, -1)
    wq2 = jnp.concatenate([wq_n, wq_r, wq_rs], axis=1).astype(BF16)

    wkv = w_kvb[layer].reshape(KV_LORA, MLA_HEADS, QK_NOPE + V_HEAD)
    wkv2 = jnp.concatenate([wkv[:, :, :QK_NOPE].reshape(KV_LORA, -1),
                            wkv[:, :, QK_NOPE:].reshape(KV_LORA, -1)], axis=1).astype(BF16)

    def qk_gain(g):
        g_r = g[QK_NOPE:]
        return jnp.stack([g[:QK_NOPE], _pad_lanes(g_r), _pad_lanes(_swap_halves(g_r))]).astype(F32)

    lower = jax.nn.softmax(lb_param.astype(F32), axis=0)[layer]
    row = lambda g: g.reshape(1, -1).astype(F32)

    x2 = x.reshape(t, D_MODEL)
    q, k, v, hq, hv, hff, hfb, hg = _in_proj(
        x2, cos_t, sin_t, row(g_mix[layer]), w_lat, w_kr, w_hg, row(g_qa[layer]), row(g_kva[layer]), wq2, wkv2,
        qk_gain(g_qn[layer]), qk_gain(g_kn[layer]), tm=IN_PROJ_ROWS)

    a = _attention(q.reshape(bsz, seq, -1), k.reshape(bsz, seq, -1), v.reshape(bsz, seq, -1),
                   tq=ATTN_Q_BLOCK, tk=ATTN_KEY_CHUNK, unroll=ATTN_UNROLL, heads=1)

    tt = np.arange(CHUNK)
    tri = np.stack([tt[:, None] >= tt[None, :], tt[:, None] <= tt[None, :]])
    tri = jnp.asarray(np.concatenate([tri, tri], axis=-1), dtype=BF16)
    r3 = lambda z: z.reshape(bsz, seq, HG_W)
    o_f, o_b = _hgrn(r3(hq), r3(hv), r3(hff), r3(hfb), lower, tri, blk=HGRN_BLOCK)

    out = _out_ffn_ple(
        x2, a.reshape(t, -1), o_f.reshape(t, HG_W), o_b.reshape(t, HG_W), hg, p[layer].reshape(t, PLE_DIM),
        row(g_hgo[layer]), w_o[layer].astype(BF16), row(g_ffn[layer]),
        w_gate[layer].astype(BF16), w_up[layer].astype(BF16), w_down[layer].astype(BF16),
        row(g_ple[layer]), w_ple_gate[layer].astype(BF16), w_ple_proj[layer].astype(BF16),
        tm=OUT_ROWS)
    return out.reshape(bsz, seq, D_MODEL)
```

```python
import functools

import jax
import jax.numpy as jnp
import numpy as np
from jax import lax
from jax.experimental import pallas as pl
from jax.experimental.pallas import tpu as pltpu

D_MODEL = 1024
MLA_HEADS = 4
QK_NOPE = 128
QK_ROPE = 64
V_HEAD = 128
Q_LORA = 256
KV_LORA = 256
QK_HEAD = QK_NOPE + QK_ROPE
HG_HEADS = 4
HG_DK = 128
HG_DV = 128
HG_W = HG_HEADS * HG_DK
CHUNK = 64
D_FF = 2816
PLE_DIM = 256
ROPE_THETA = 10000.0
EPS = 1e-6

LANES = 128
SUBLANES = 8
MXU_TILE = 256
QK_PAD = MXU_TILE
VMEM_LIMIT = 56 * 1024 * 1024

IN_PROJ_ROWS = 512
OUT_ROWS = 512
FF_CHUNK = MXU_TILE
ATTN_Q_BLOCK = MXU_TILE
ATTN_KEY_CHUNK = 2048
ATTN_UNROLL = 6
HGRN_BLOCK = 1024

BF16 = jnp.bfloat16
F32 = jnp.float32


def _const_spec(shape):
    return pl.BlockSpec(shape, lambda *_: (0,) * len(shape), pipeline_mode=pl.Buffered(1))


def _sigmoid(x):
    return 0.5 + 0.5 * jnp.tanh(0.5 * x)


def _silu(x):
    h = 0.5 * x
    return h + h * jnp.tanh(h)


def _rms_scale(ss, n):
    return lax.rsqrt(ss * (1.0 / n) + EPS)


N_FREQ = QK_ROPE // 2
TOK_PER_ROW = LANES // N_FREQ


def _split3(x):
    p1 = x.astype(BF16)
    r1 = x - p1.astype(F32)
    p2 = r1.astype(BF16)
    p3 = (r1 - p2.astype(F32)).astype(BF16)
    return p1, p2, p3


def _rope_kernel(pos_ref, freq_ref, selc_ref, sels_ref, cos_ref, sin_ref):
    rows = pos_ref.shape[0]
    pos = pos_ref[...].astype(F32)
    lane = lax.broadcasted_iota(jnp.int32, (rows, LANES), 1)
    pos_d = jnp.broadcast_to(pos[:, TOK_PER_ROW - 1:TOK_PER_ROW], (rows, LANES))
    for i in range(TOK_PER_ROW - 2, -1, -1):
        pos_d = jnp.where(lane < (i + 1) * N_FREQ, jnp.broadcast_to(pos[:, i:i + 1], (rows, LANES)), pos_d)
    ang = pos_d * freq_ref[...]
    for trig, sel_ref, out_ref in ((jnp.cos(ang), selc_ref, cos_ref), (jnp.sin(ang), sels_ref, sin_ref)):
        pieces = jnp.concatenate(_split3(trig), axis=-1)
        for i in range(TOK_PER_ROW):
            out_ref[pl.ds(i, rows, stride=TOK_PER_ROW), :] = jnp.dot(
                pieces, sel_ref[i], preferred_element_type=F32)


def _rope_select_matrices():
    selc = np.zeros((TOK_PER_ROW, 3 * LANES, LANES), np.float32)
    sels = np.zeros_like(selc)
    j = np.arange(N_FREQ)
    for i in range(TOK_PER_ROW):
        for piece in range(3):
            for m in range(LANES // N_FREQ):
                selc[i, piece * LANES + N_FREQ * i + j, N_FREQ * m + j] = 1.0
                sels[i, piece * LANES + N_FREQ * i + j, N_FREQ * m + j] = 1.0 if m % 2 else -1.0
    return jnp.asarray(selc, BF16), jnp.asarray(sels, BF16)


def _rope_tables(pos4, freq_row):
    rows = pos4.shape[0]
    blk = 1024
    selc, sels = _rope_select_matrices()
    out = pl.BlockSpec((blk * TOK_PER_ROW, LANES), lambda i: (i, 0))
    return pl.pallas_call(
        _rope_kernel,
        grid=(rows // blk,),
        in_specs=[pl.BlockSpec((blk, TOK_PER_ROW), lambda i: (i, 0)), _const_spec(freq_row.shape),
                  _const_spec(selc.shape), _const_spec(sels.shape)],
        out_specs=[out, out],
        out_shape=[jax.ShapeDtypeStruct((rows * TOK_PER_ROW, LANES), F32)] * 2,
        compiler_params=pltpu.CompilerParams(dimension_semantics=("arbitrary",)),
        name="rope_tables",
    )(pos4, freq_row, selc, sels)


def _in_proj_kernel(x_ref, cos_ref, sin_ref, gmix_ref, wt_ref, wkr_ref, gqa_ref, gkva_ref,
                    wq2_ref, wkv2_ref, gq_ref, gk_ref,
                    q_ref, k_ref, v_ref, hq_ref, hv_ref, hff_ref, hfb_ref, hg_ref):
    x = x_ref[...]
    h = (x * _rms_scale(jnp.sum(x * x, axis=-1, keepdims=True), D_MODEL) * gmix_ref[...]).astype(BF16)
    nt = (((1,), (1,)), ((), ()))
    lat = Q_LORA + KV_LORA
    z_lat = lax.dot_general(h, wt_ref[:lat, :], nt, preferred_element_type=F32)
    z_kr = lax.dot_general(h, wkr_ref[...], nt, preferred_element_type=F32)
    z_hg = lax.dot_general(h, wt_ref[lat + QK_ROPE:, :], nt, preferred_element_type=F32)

    c_q = z_lat[:, 0:Q_LORA]
    c_kv = z_lat[:, Q_LORA:Q_LORA + KV_LORA]
    hq = z_hg[:, 0:HG_W]
    hff = z_hg[:, HG_W:2 * HG_W]
    hfb = z_hg[:, 2 * HG_W:3 * HG_W]
    hi = z_hg[:, 3 * HG_W:4 * HG_W]
    hg = z_hg[:, 4 * HG_W:5 * HG_W]
    k_r = z_kr[:, 0:LANES]
    k_rs = z_kr[:, LANES:2 * LANES]

    cos = cos_ref[...]
    sin = sin_ref[...]

    cqn = c_q * _rms_scale(jnp.sum(c_q * c_q, axis=-1, keepdims=True), Q_LORA) * gqa_ref[...]
    q_all = jnp.dot(cqn.astype(BF16), wq2_ref[...], preferred_element_type=F32)
    gq_n, gq_r, gq_rs = gq_ref[0:1, :], gq_ref[1:2, :], gq_ref[2:3, :]
    scale = QK_HEAD ** -0.5 * np.log2(np.e)
    nh = MLA_HEADS
    for hd in range(nh):
        qn = q_all[:, hd * LANES:(hd + 1) * LANES]
        qr = q_all[:, (nh + hd) * LANES:(nh + hd + 1) * LANES]
        qrs = q_all[:, (2 * nh + hd) * LANES:(2 * nh + hd + 1) * LANES]
        ss = jnp.sum(qn * qn + qr * qr, axis=-1, keepdims=True)
        r = _rms_scale(ss, QK_HEAD) * scale
        q_ref[:, hd * QK_PAD:hd * QK_PAD + LANES] = (qn * gq_n * r).astype(BF16)
        q_ref[:, hd * QK_PAD + LANES:(hd + 1) * QK_PAD] = (
            (qr * gq_r * cos + qrs * gq_rs * sin) * r).astype(BF16)

    ckvn = c_kv * _rms_scale(jnp.sum(c_kv * c_kv, axis=-1, keepdims=True), KV_LORA) * gkva_ref[...]
    kv_all = jnp.dot(ckvn.astype(BF16), wkv2_ref[...], preferred_element_type=F32)
    gk_n, gk_r, gk_rs = gk_ref[0:1, :], gk_ref[1:2, :], gk_ref[2:3, :]
    k_rot = k_r * gk_r * cos + k_rs * gk_rs * sin
    ss_r = jnp.sum(k_r * k_r, axis=-1, keepdims=True)
    for hd in range(nh):
        kn = kv_all[:, hd * LANES:(hd + 1) * LANES]
        ss = jnp.sum(kn * kn, axis=-1, keepdims=True) + ss_r
        r = _rms_scale(ss, QK_HEAD)
        k_ref[:, hd * QK_PAD:hd * QK_PAD + LANES] = (kn * gk_n * r).astype(BF16)
        k_ref[:, hd * QK_PAD + LANES:(hd + 1) * QK_PAD] = (k_rot * r).astype(BF16)
    v_ref[...] = kv_all[:, nh * LANES:].astype(BF16)

    hq_ref[...] = _silu(hq)
    hv_ref[...] = hi.astype(BF16)
    hff_ref[...] = hff
    hfb_ref[...] = hfb
    hg_ref[...] = _silu(hg).astype(BF16)


def _in_proj(x2, cos_t, sin_t, *consts, tm):
    t = x2.shape[0]
    row = lambda w: pl.BlockSpec((tm, w), lambda i: (i, 0))
    outs = [(MLA_HEADS * QK_PAD, BF16), (MLA_HEADS * QK_PAD, BF16), (MLA_HEADS * V_HEAD, BF16),
            (HG_W, F32), (HG_W, BF16), (HG_W, F32), (HG_W, F32), (HG_W, BF16)]
    return pl.pallas_call(
        _in_proj_kernel,
        grid=(t // tm,),
        in_specs=[row(D_MODEL), row(LANES), row(LANES)] + [_const_spec(c.shape) for c in consts],
        out_specs=[row(w) for w, _ in outs],
        out_shape=[jax.ShapeDtypeStruct((t, w), dt) for w, dt in outs],
        compiler_params=pltpu.CompilerParams(dimension_semantics=("arbitrary",),
                                             vmem_limit_bytes=VMEM_LIMIT),
        name="in_proj",
    )(x2, cos_t, sin_t, *consts)


def _attn_kernel(q_ref, k_ref, v_ref, o_ref, s0_ref, s1_ref, vt_ref, *, tq, tk, unroll, heads):
    seq = k_ref.shape[0]
    nq, nk = seq // tq, seq // tk
    nt = (((1,), (1,)), ((), ()))
    sub = SUBLANES

    def fold(x, op):
        return op(x.reshape(tk // sub, sub, tq), axis=0)

    def q_rows(i):
        start = i * tq
        return pl.ds(start if isinstance(start, int) else pl.multiple_of(start, tq), tq)

    for hd in range(heads):
        vt_ref[hd] = v_ref[:, hd * V_HEAD:(hd + 1) * V_HEAD].astype(F32).T.astype(BF16)

    def step(nxt, s_w, cur, s_r, m_cur):
        if nxt is not None:
            qk_n = slice(nxt[0] * QK_PAD, (nxt[0] + 1) * QK_PAD)
            q = q_ref[q_rows(nxt[1]), qk_n]
            mrun = jnp.full((sub, tq), -jnp.inf, F32)
        if cur is not None:
            lrun = jnp.zeros((sub, tq), F32)
            acc = jnp.zeros((V_HEAD, tq), F32)
        for kc in range(nk):
            ks = slice(kc * tk, (kc + 1) * tk)
            if nxt is not None:
                s_c = lax.dot_general(k_ref[ks, qk_n], q, nt, preferred_element_type=F32)
                s_w[ks, :] = s_c
                mrun = jnp.maximum(mrun, fold(s_c, jnp.max))
            if cur is not None:
                p_c = jnp.exp2(s_r[ks, :].reshape(tk // sub, sub, tq) - m_cur[None]).reshape(tk, tq)
                lrun = lrun + fold(p_c, jnp.sum)
                acc = acc + jnp.dot(vt_ref[cur[0], :, ks], p_c.astype(BF16), preferred_element_type=F32)
        if cur is not None:
            l = jnp.sum(lrun, axis=0, keepdims=True)
            o_ref[q_rows(cur[1]), cur[0] * V_HEAD:(cur[0] + 1) * V_HEAD] = (acc / l).T.astype(o_ref.dtype)
        if nxt is None:
            return None
        return jnp.broadcast_to(jnp.max(mrun, axis=0, keepdims=True), (sub, tq))

    bufs = (s0_ref, s1_ref)
    m = step((0, 0), bufs[0], None, None, None)
    for hd in range(heads):
        def run_steps(first, count, m, hd=hd):
            for u in range(count):
                m = step((hd, first + u + 1), bufs[(u + 1) % 2], (hd, first + u), bufs[u % 2], m)
            return m

        trips = (nq - 1) // unroll
        m = lax.fori_loop(0, trips, lambda j, m, run=run_steps: run(j * unroll, unroll, m), m)
        m = run_steps(trips * unroll, nq - 1 - trips * unroll, m)
        nxt = (hd + 1, 0) if hd + 1 < heads else None
        m = step(nxt, bufs[0], (hd, nq - 1), bufs[1], m)


def _attention(q, k, v, tq, tk, unroll, heads):
    b, s, _ = q.shape
    assert s % (2 * tq) == 0 and s % tk == 0 and tk % LANES == 0 and unroll % 2 == 0
    assert MLA_HEADS % heads == 0
    return pl.pallas_call(
        functools.partial(_attn_kernel, tq=tq, tk=tk, unroll=unroll, heads=heads),
        grid=(b, MLA_HEADS // heads),
        in_specs=[pl.BlockSpec((None, s, heads * QK_PAD), lambda bi, gi: (bi, 0, gi)),
                  pl.BlockSpec((None, s, heads * QK_PAD), lambda bi, gi: (bi, 0, gi)),
                  pl.BlockSpec((None, s, heads * V_HEAD), lambda bi, gi: (bi, 0, gi))],
        out_specs=pl.BlockSpec((None, s, heads * V_HEAD), lambda bi, gi: (bi, 0, gi)),
        out_shape=jax.ShapeDtypeStruct((b, s, MLA_HEADS * V_HEAD), BF16),
        scratch_shapes=[pltpu.VMEM((s, tq), F32), pltpu.VMEM((s, tq), F32),
                        pltpu.VMEM((heads, V_HEAD, s), BF16)],
        compiler_params=pltpu.CompilerParams(dimension_semantics=("arbitrary", "arbitrary"),
                                             vmem_limit_bytes=VMEM_LIMIT),
        name="attention",
    )(q, k, v)


def _hgrn_kernel(qf_ref, vf_ref, ff_ref, qb_ref, vb_ref, fb_ref, lb_ref, tri_ref,
                 of_ref, ob_ref, st_ref, *, n_chunks):
    @pl.when(pl.program_id(1) == 0)
    def _():
        st_ref[...] = jnp.zeros_like(st_ref)

    t_idx = lax.broadcasted_iota(jnp.int32, (CHUNK, CHUNK), 0)
    s_idx = lax.broadcasted_iota(jnp.int32, (CHUNK, CHUNK), 1)
    streams = (
        (qf_ref, vf_ref, ff_ref, None, of_ref, t_idx >= s_idx, CHUNK // 2 - 1, CHUNK - 1),
        (qb_ref, vb_ref, fb_ref, None, ob_ref, t_idx <= s_idx, CHUNK // 2, 0),
    )
    gate_half = [0.5 * (1.0 - lb_ref[d:d + 1, :]) for d in range(2)]
    gate_mid = [0.5 * (1.0 + lb_ref[d:d + 1, :]) for d in range(2)]

    nt = (((1,), (1,)), ((), ()))
    tn = (((0,), (0,)), ((), ()))
    heads = [slice(hd * HG_DK, (hd + 1) * HG_DK) for hd in range(HG_HEADS)]

    def chunk_of(d, i):
        return i if d == 0 else n_chunks - 1 - i

    cum, gated, intra = {}, {}, {}
    state = {(d, hd): st_ref[d, hd] for d in range(2) for hd in range(HG_HEADS)}

    def stage_cumsum(d, i):
        f_ref = streams[d][2]
        rows = slice(chunk_of(d, i) * CHUNK, (chunk_of(d, i) + 1) * CHUNK)
        ct = gate_half[d] * jnp.tanh(0.5 * f_ref[rows, :])
        k = gate_half[d] - ct
        lf = jnp.log2(gate_mid[d] + ct)
        lf_hi = lf.astype(BF16)
        lf_lo = (lf - lf_hi.astype(F32)).astype(BF16)
        hi_lo = jnp.concatenate([lf_hi, lf_lo], axis=0)
        cum[d, i] = (jnp.dot(tri_ref[d], hi_lo, preferred_element_type=F32), k)

    def stage_gates(d, i):
        q_ref, _, _, _, _, _, mid, last = streams[d]
        rows = slice(chunk_of(d, i) * CHUNK, (chunk_of(d, i) + 1) * CHUNK)
        b, k = cum.pop((d, i))
        b_mid = b[mid:mid + 1, :]
        b_last = b[last:last + 1, :]
        e_mid = jnp.exp2(b - b_mid)
        q_in = q_ref[rows, :] * e_mid
        k_in = k / e_mid
        q_st = (q_in * jnp.exp2(b_mid)).astype(BF16)
        k_st = (k_in * jnp.exp2(b_last - b_mid)).astype(BF16)
        gated[d, i] = (q_in.astype(BF16), k_in.astype(BF16), q_st, k_st, jnp.exp2(b_last))

    def stage_intra(d, i):
        v_ref, mask = streams[d][1], streams[d][5]
        rows = slice(chunk_of(d, i) * CHUNK, (chunk_of(d, i) + 1) * CHUNK)
        q_in, k_in, q_st, k_st, decay = gated.pop((d, i))
        v = v_ref[rows, :]
        a, kv_t = [], []
        for sl in heads:
            a_h = lax.dot_general(q_in[:, sl], k_in[:, sl], nt, preferred_element_type=F32)
            a.append(jnp.where(mask, a_h, 0.0).astype(BF16))
            kv_t.append(lax.dot_general(v[:, sl], k_st[:, sl], tn, preferred_element_type=F32))
        intra[d, i] = (rows, v, q_st, a, kv_t, decay)

    def stage_state(d, i):
        o_ref = streams[d][4]
        rows, v, q_st, a, kv_t, decay = intra.pop((d, i))
        for hd, sl in enumerate(heads):
            st = state[d, hd]
            o = jnp.dot(a[hd], v[:, sl], preferred_element_type=F32)
            o += jnp.dot(q_st[:, sl], st.T.astype(BF16), preferred_element_type=F32)
            o_ref[rows, sl] = o
            state[d, hd] = st * decay[:, sl] + kv_t[hd]

    stages = (stage_cumsum, stage_gates, stage_intra, stage_state)
    for t in range(n_chunks + len(stages) - 1):
        for lag, stage in enumerate(stages):
            if 0 <= t - lag < n_chunks:
                for d in range(2):
                    stage(d, t - lag)
    for (d, hd), st in state.items():
        st_ref[d, hd] = st


def _hgrn(hq, hv, hff, hfb, lb, tri, blk):
    b, s, _ = hq.shape
    nb = s // blk
    fwd = pl.BlockSpec((None, blk, HG_W), lambda bi, j: (bi, j, 0))
    bwd = pl.BlockSpec((None, blk, HG_W), lambda bi, j: (bi, nb - 1 - j, 0))
    return pl.pallas_call(
        functools.partial(_hgrn_kernel, n_chunks=blk // CHUNK),
        grid=(b, nb),
        in_specs=[fwd] * 3 + [bwd] * 3 + [_const_spec(lb.shape), _const_spec(tri.shape)],
        out_specs=[fwd, bwd],
        out_shape=[jax.ShapeDtypeStruct((b, s, HG_W), F32)] * 2,
        scratch_shapes=[pltpu.VMEM((2, HG_HEADS, HG_DV, HG_DK), F32)],
        compiler_params=pltpu.CompilerParams(dimension_semantics=("arbitrary", "arbitrary"),
                                             vmem_limit_bytes=VMEM_LIMIT),
        name="hgrn2",
    )(hq, hv, hff, hq, hv, hfb, lb, tri)


def _out_kernel(x_ref, a_ref, of_ref, ob_ref, hg_ref, p_ref, ghgo_ref, wo_ref, gffn_ref,
                wg_ref, wu_ref, wd_ref, gple_ref, wpg_ref, wpp_ref, out_ref, *, ff_chunk):
    o = of_ref[...] + ob_ref[...]
    parts = []
    for hd in range(HG_HEADS):
        oh = o[:, hd * HG_DV:(hd + 1) * HG_DV]
        parts.append(oh * _rms_scale(jnp.sum(oh * oh, axis=-1, keepdims=True), HG_DV))
    r = jnp.concatenate(parts, axis=-1) * ghgo_ref[...] * hg_ref[...].astype(F32)
    x = (x_ref[...]
         + jnp.dot(a_ref[...], wo_ref[:MLA_HEADS * V_HEAD, :], preferred_element_type=F32)
         + jnp.dot(r.astype(BF16), wo_ref[MLA_HEADS * V_HEAD:, :], preferred_element_type=F32))

    h = (x * _rms_scale(jnp.sum(x * x, axis=-1, keepdims=True), D_MODEL) * gffn_ref[...]).astype(BF16)
    ffn = None
    for c in range(D_FF // ff_chunk):
        cs = slice(c * ff_chunk, (c + 1) * ff_chunk)
        gate = jnp.dot(h, wg_ref[:, cs], preferred_element_type=F32)
        up = jnp.dot(h, wu_ref[:, cs], preferred_element_type=F32)
        act = (_silu(gate) * up).astype(BF16)
        down = jnp.dot(act, wd_ref[cs, :], preferred_element_type=F32)
        ffn = down if ffn is None else ffn + down
    x = x + ffn

    h = (x * _rms_scale(jnp.sum(x * x, axis=-1, keepdims=True), D_MODEL) * gple_ref[...]).astype(BF16)
    gate = _sigmoid(jnp.dot(h, wpg_ref[...], preferred_element_type=F32))
    proj = jnp.dot(p_ref[...].astype(BF16), wpp_ref[...], preferred_element_type=F32)
    out_ref[...] = x + gate * proj


def _out_ffn_ple(x2, a, o_f, o_b, hg, p2, ghgo, wo, gffn, wg, wu, wd, gple, wpg, wpp, tm):
    t = x2.shape[0]
    row = lambda w: pl.BlockSpec((tm, w), lambda i: (i, 0))
    consts = (ghgo, wo, gffn, wg, wu, wd, gple, wpg, wpp)
    return pl.pallas_call(
        functools.partial(_out_kernel, ff_chunk=FF_CHUNK),
        grid=(t // tm,),
        in_specs=[row(D_MODEL), row(HG_W), row(HG_W), row(HG_W), row(HG_W), row(PLE_DIM)]
                 + [_const_spec(c.shape) for c in consts],
        out_specs=row(D_MODEL),
        out_shape=jax.ShapeDtypeStruct((t, D_MODEL), F32),
        compiler_params=pltpu.CompilerParams(dimension_semantics=("arbitrary",),
                                             vmem_limit_bytes=VMEM_LIMIT),
        name="out_ffn_ple",
    )(x2, a, o_f, o_b, hg, p2, *consts)


def _swap_halves(w):
    half = w.shape[-1] // 2
    return jnp.concatenate([w[..., half:], w[..., :half]], axis=-1)


def _pad_lanes(w):
    return jnp.pad(w, [(0, 0)] * (w.ndim - 1) + [(0, LANES - w.shape[-1])])


def kernel(x, p, positions, g_mix, w_in, g_qa, g_kva, w_qb, w_kvb, g_qn, g_kn, lb_param, g_hgo,
           w_o, g_ffn, w_gate, w_up, w_down, g_ple, w_ple_gate, w_ple_proj):
    bsz, seq, _ = x.shape
    t = bsz * seq
    layer = 0

    inv_freq = (ROPE_THETA ** (-np.arange(0, QK_ROPE, 2, dtype=np.float64) / QK_ROPE)).astype(np.float32)
    freq_row = jnp.asarray(np.tile(inv_freq, TOK_PER_ROW)[None, :])
    cos_t, sin_t = _rope_tables(positions.reshape(t // TOK_PER_ROW, TOK_PER_ROW), freq_row)

    wi_t = jnp.swapaxes(w_in[layer], 0, 1).astype(BF16)
    lat = Q_LORA + KV_LORA
    k_r = wi_t[lat:lat + QK_ROPE]
    gap = jnp.zeros((LANES - QK_ROPE, D_MODEL), BF16)
    k_rs = jnp.concatenate([k_r[QK_ROPE // 2:], k_r[:QK_ROPE // 2]], axis=0)
    w_kr = jnp.concatenate([k_r, gap, k_rs, gap], axis=0)

    wq = w_qb[layer].reshape(Q_LORA, MLA_HEADS, QK_HEAD)
    wq_n = wq[:, :, :QK_NOPE].reshape(Q_LORA, -1)
    wq_r = _pad_lanes(wq[:, :, QK_NOPE:]).reshape(Q_LORA, -1)
    wq_rs = _pad_lanes(_swap_halves(wq[:, :, QK_NOPE:])).reshape(Q_LORA, -1)
    wq2 = jnp.concatenate([wq_n, wq_r, wq_rs], axis=1).astype(BF16)

    wkv = w_kvb[layer].reshape(KV_LORA, MLA_HEADS, QK_NOPE + V_HEAD)
    wkv2 = jnp.concatenate([wkv[:, :, :QK_NOPE].reshape(KV_LORA, -1),
                            wkv[:, :, QK_NOPE:].reshape(KV_LORA, -1)], axis=1).astype(BF16)

    def qk_gain(g):
        g_r = g[QK_NOPE:]
        return jnp.stack([g[:QK_NOPE], _pad_lanes(g_r), _pad_lanes(_swap_halves(g_r))]).astype(F32)

    lower = jax.nn.softmax(lb_param.astype(F32), axis=0)[layer]
    row = lambda g: g.reshape(1, -1).astype(F32)

    x2 = x.reshape(t, D_MODEL)
    q, k, v, hq, hv, hff, hfb, hg = _in_proj(
        x2, cos_t, sin_t, row(g_mix[layer]), wi_t, w_kr, row(g_qa[layer]), row(g_kva[layer]), wq2, wkv2,
        qk_gain(g_qn[layer]), qk_gain(g_kn[layer]), tm=IN_PROJ_ROWS)

    a = _attention(q.reshape(bsz, seq, -1), k.reshape(bsz, seq, -1), v.reshape(bsz, seq, -1),
                   tq=ATTN_Q_BLOCK, tk=ATTN_KEY_CHUNK, unroll=ATTN_UNROLL, heads=1)

    tt = np.arange(CHUNK)
    tri = np.stack([tt[:, None] >= tt[None, :], tt[:, None] <= tt[None, :]])
    tri = jnp.asarray(np.concatenate([tri, tri], axis=-1), dtype=BF16)
    r3 = lambda z: z.reshape(bsz, seq, HG_W)
    o_f, o_b = _hgrn(r3(hq), r3(hv), r3(hff), r3(hfb), lower, tri, blk=HGRN_BLOCK)

    out = _out_ffn_ple(
        x2, a.reshape(t, -1), o_f.reshape(t, HG_W), o_b.reshape(t, HG_W), hg, p[layer].reshape(t, PLE_DIM),
        row(g_hgo[layer]), w_o[layer].astype(BF16), row(g_ffn[layer]),
        w_gate[layer].astype(BF16), w_up[layer].astype(BF16), w_down[layer].astype(BF16),
        row(g_ple[layer]), w_ple_gate[layer].astype(BF16), w_ple_proj[layer].astype(BF16),
        tm=OUT_ROWS)
    return out.reshape(bsz, seq, D_MODEL)
```
